```python
import jax, jax.numpy as jnp
from jax import lax
import numpy as np

D_MODEL = 2048
BATCH = 2
SEQ = 4096
DEPTH = 1

CHUNK = 64
Q_BLOCK = 2 * CHUNK
POOL_WIDTH = D_MODEL // 2
POOL_WINDOWS = (2, 4, 8, 16)
POOL_GROUPS = len(POOL_WINDOWS)
POOL_GROUP_WIDTH = POOL_WIDTH // POOL_GROUPS
SB_WIDTH = D_MODEL - POOL_WIDTH
SB_HEAD_DIM = 128
SB_HEADS = SB_WIDTH // SB_HEAD_DIM
MIX_WIDTH = POOL_WIDTH + SB_WIDTH
IN_PROJ_WIDTH = POOL_WIDTH + 3 * SB_WIDTH
D_FF = 4 * D_MODEL
DEEPNORM_ALPHA = (2.0 * DEPTH) ** 0.25
DEEPNORM_BETA = (8.0 * DEPTH) ** -0.25
LN_EPS = 1e-5

kernel_name = "hybrid_pool_stickbreaking_deepnorm_block"


def layer_norm(x, g, b):
    xf = x.astype(jnp.float32)
    mu = jnp.mean(xf, axis=-1, keepdims=True)
    var = jnp.mean(jnp.square(xf - mu), axis=-1, keepdims=True)
    y = (xf - mu) * lax.rsqrt(var + LN_EPS)
    return (y * g.astype(jnp.float32) + b.astype(jnp.float32)).astype(x.dtype)


def multi_scale_pool(u, w_pool, pool_scale):
    b, s, _ = u.shape
    ug = u.reshape(b, s, POOL_GROUPS, POOL_GROUP_WIDTH).astype(jnp.float32)
    csum = jnp.concatenate(
        [jnp.zeros((b, 1, POOL_GROUPS, POOL_GROUP_WIDTH), jnp.float32),
         jnp.cumsum(ug, axis=1)], axis=1)
    t = jnp.arange(s, dtype=jnp.int32)
    windows = jnp.asarray(POOL_WINDOWS, dtype=jnp.int32)
    start = jnp.maximum(t[:, None] + 1 - windows[None, :], 0)
    count = (t[:, None] + 1 - start).astype(jnp.float32)
    group_idx = jnp.arange(POOL_GROUPS, dtype=jnp.int32)[None, :]
    c_start = csum[:, start, group_idx]
    mean = (csum[:, 1:] - c_start) / count[None, :, :, None]
    y = mean - ug
    y = jnp.einsum('bsgc,gcd->bsgd', y, w_pool.astype(jnp.float32))
    y = y * pool_scale.astype(jnp.float32)[None, None]
    return y.reshape(b, s, POOL_WIDTH).astype(u.dtype)


def stick_breaking_attention(q, k, v):
    b, s, h, dh = q.shape
    n_blocks = s // Q_BLOCK
    scale = 1.0 / np.sqrt(dh).astype(np.float32)
    qb = q.reshape(b, n_blocks, Q_BLOCK, h, dh).transpose(1, 0, 2, 3, 4)
    key_pos = jnp.arange(s, dtype=jnp.int32)

    def one_block(args):
        qi, i = args
        z = jnp.einsum('bqhd,bkhd->bhqk', qi, k).astype(jnp.float32) * scale
        q_pos = i * Q_BLOCK + jnp.arange(Q_BLOCK, dtype=jnp.int32)
        mask = (key_pos[None, :] < q_pos[:, None])[None, None]
        log_not = jnp.where(mask, jax.nn.log_sigmoid(-z), 0.0)
        after = lax.cumsum(log_not, axis=3, reverse=True) - log_not
        a = jnp.where(mask, jnp.exp(jax.nn.log_sigmoid(z) + after), 0.0)
        return jnp.einsum('bhqk,bkhd->bqhd', a.astype(v.dtype), v)

    out = lax.map(one_block, (qb, jnp.arange(n_blocks, dtype=jnp.int32)))
    return out.transpose(1, 0, 2, 3, 4).reshape(b, s, h * dh)


def setup_inputs(seed: int = 0) -> dict:
    key = jax.random.key(seed)
    ks = jax.random.split(key, 16)
    f32 = jnp.float32
    x = jax.random.normal(ks[0], (BATCH, SEQ, D_MODEL), f32)
    ln_in_g = 1.0 + 0.02 * jax.random.normal(ks[1], (D_MODEL,), f32)
    ln_in_b = 0.02 * jax.random.normal(ks[2], (D_MODEL,), f32)
    w_in = jax.random.normal(ks[3], (DEPTH, D_MODEL, IN_PROJ_WIDTH), f32) * D_MODEL ** -0.5
    w_pool = jax.random.normal(ks[4], (DEPTH, POOL_GROUPS, POOL_GROUP_WIDTH, POOL_GROUP_WIDTH), f32) * POOL_GROUP_WIDTH ** -0.5
    pool_scale = 1.0 + 0.02 * jax.random.normal(ks[5], (DEPTH, POOL_GROUPS, POOL_GROUP_WIDTH), f32)
    w_out = jax.random.normal(ks[6], (DEPTH, MIX_WIDTH, D_MODEL), f32) * (MIX_WIDTH ** -0.5 * DEEPNORM_BETA)
    ln1_g = 1.0 + 0.02 * jax.random.normal(ks[7], (DEPTH, D_MODEL), f32)
    ln1_b = 0.02 * jax.random.normal(ks[8], (DEPTH, D_MODEL), f32)
    w_ff1 = jax.random.normal(ks[9], (DEPTH, D_MODEL, D_FF), f32) * D_MODEL ** -0.5
    b_ff1 = 0.02 * jax.random.normal(ks[10], (DEPTH, D_FF), f32)
    w_ff2 = jax.random.normal(ks[11], (DEPTH, D_FF, D_MODEL), f32) * (D_FF ** -0.5 * DEEPNORM_BETA)
    b_ff2 = 0.02 * jax.random.normal(ks[12], (DEPTH, D_MODEL), f32)
    ln2_g = 1.0 + 0.02 * jax.random.normal(ks[13], (DEPTH, D_MODEL), f32)
    ln2_b = 0.02 * jax.random.normal(ks[14], (DEPTH, D_MODEL), f32)
    return {"x": x, "ln_in_g": ln_in_g, "ln_in_b": ln_in_b, "w_in": w_in,
            "w_pool": w_pool, "pool_scale": pool_scale, "w_out": w_out,
            "ln1_g": ln1_g, "ln1_b": ln1_b, "w_ff1": w_ff1, "b_ff1": b_ff1,
            "w_ff2": w_ff2, "b_ff2": b_ff2, "ln2_g": ln2_g, "ln2_b": ln2_b}


def reference(x, ln_in_g, ln_in_b, w_in, w_pool, pool_scale, w_out,
              ln1_g, ln1_b, w_ff1, b_ff1, w_ff2, b_ff2, ln2_g, ln2_b):
    b, s, _ = x.shape
    h = layer_norm(x, ln_in_g, ln_in_b)
    for layer in range(DEPTH):
        u = jnp.einsum('bsd,de->bse', h, w_in[layer])
        u_pool = u[..., :POOL_WIDTH]
        q, k, v = jnp.split(u[..., POOL_WIDTH:], 3, axis=-1)
        q = q.reshape(b, s, SB_HEADS, SB_HEAD_DIM)
        k = k.reshape(b, s, SB_HEADS, SB_HEAD_DIM)
        v = v.reshape(b, s, SB_HEADS, SB_HEAD_DIM)
        y_pool = multi_scale_pool(u_pool, w_pool[layer], pool_scale[layer])
        y_sb = stick_breaking_attention(q, k, v)
        mix = jnp.concatenate([y_pool, y_sb], axis=-1)
        mix = jnp.einsum('bse,ed->bsd', mix, w_out[layer])
        h = layer_norm(DEEPNORM_ALPHA * h + mix, ln1_g[layer], ln1_b[layer])
        f = jnp.einsum('bsd,df->bsf', h, w_ff1[layer]) + b_ff1[layer]
        f = jnp.square(jax.nn.relu(f))
        f = jnp.einsum('bsf,fd->bsd', f, w_ff2[layer]) + b_ff2[layer]
        h = layer_norm(DEEPNORM_ALPHA * h + f, ln2_g[layer], ln2_b[layer])
    return h
```

```python
import functools

import jax
import jax.numpy as jnp
import numpy as np
from jax import lax
from jax.experimental import pallas as pl
from jax.experimental.pallas import tpu as pltpu

F32 = jnp.float32
BF16 = jnp.bfloat16

LN_EPS = 1e-5
POOL_WINDOWS = (2, 4, 8, 16)
POOL_HALO = 16
HEAD_DIM = 128

VMEM_LIMIT_BYTES = 56 * 1024 * 1024

INPROJ_TM = 1024
INPROJ_TN = 1024
ATTN_TQ = 128
ATTN_TK = 256
MIX_TM = 512
FFN_TM = 512
FFN_TF = 1024
LN_ROWS = 128


def _layer_norm(x, g, b):
    mu = jnp.mean(x, axis=-1, keepdims=True)
    xc = x - mu
    var = jnp.mean(xc * xc, axis=-1, keepdims=True)
    return xc * lax.rsqrt(var + LN_EPS) * g + b


def _inproj_kernel(x_ref, g_ref, b_ref, w_ref, pool_ref, qkv_ref, h_scr):
    j = pl.program_id(1)
    tm = x_ref.shape[0]

    @pl.when(j == 0)
    def _():
        def body(r, carry):
            rows = pl.ds(pl.multiple_of(r * LN_ROWS, LN_ROWS), LN_ROWS)
            h_scr[rows, :] = _layer_norm(x_ref[rows, :], g_ref[...], b_ref[...]).astype(BF16)
            return carry
        lax.fori_loop(0, tm // LN_ROWS, body, 0)

    acc = jnp.dot(h_scr[...], w_ref[...], preferred_element_type=F32)

    @pl.when(j == 0)
    def _():
        pool_ref[...] = acc

    @pl.when(j > 0)
    def _():
        qkv_ref[...] = acc.astype(BF16)


def _in_proj(x2, g, b, w_in_bf16, pool_width):
    m, d = x2.shape
    n = w_in_bf16.shape[1]
    tm, tn = INPROJ_TM, INPROJ_TN
    assert pool_width == tn and m % tm == 0 and n % tn == 0
    return pl.pallas_call(
        _inproj_kernel,
        grid=(m // tm, n // tn),
        in_specs=[
            pl.BlockSpec((tm, d), lambda i, j: (i, 0)),
            pl.BlockSpec((1, d), lambda i, j: (0, 0)),
            pl.BlockSpec((1, d), lambda i, j: (0, 0)),
            pl.BlockSpec((d, tn), lambda i, j: (0, j)),
        ],
        out_specs=[
            pl.BlockSpec((tm, tn), lambda i, j: (i, 0)),
            pl.BlockSpec((tm, tn), lambda i, j: (i, jnp.maximum(j - 1, 0))),
        ],
        out_shape=[
            jax.ShapeDtypeStruct((m, pool_width), F32),
            jax.ShapeDtypeStruct((m, n - pool_width), BF16),
        ],
        scratch_shapes=[pltpu.VMEM((tm, d), BF16)],
        compiler_params=pltpu.CompilerParams(
            dimension_semantics=("arbitrary", "arbitrary"),
            vmem_limit_bytes=VMEM_LIMIT_BYTES),
        name="in_proj",
    )(x2, g, b, w_in_bf16)


def _attn_kernel(q_ref, k_ref, v_ref, cum_ref, o_ref, *, scale):
    seq = q_ref.shape[0]
    tq, tk = ATTN_TQ, ATTN_TK
    tiles_per_k = tk // tq

    def tile(qi, kt, q, carry, acc, masked):
        krows = pl.ds(pl.multiple_of(kt * tk, tk), tk)
        k = k_ref[krows, :]
        v = v_ref[krows, :]
        z = lax.dot_general(q, k, (((1,), (1,)), ((), ())),
                            preferred_element_type=F32) * scale
        log_not = -(jnp.maximum(z, 0.0) + jnp.log1p(jnp.exp(-jnp.abs(z))))
        if masked:
            q_pos = qi * tq + lax.broadcasted_iota(jnp.int32, (tq, tk), 0)
            k_pos = kt * tk + lax.broadcasted_iota(jnp.int32, (tq, tk), 1)
            mask = k_pos < q_pos
            log_not = jnp.where(mask, log_not, 0.0)
        hi = log_not.astype(BF16)
        lo = (log_not - hi.astype(F32)).astype(BF16)
        sums = jnp.dot(jnp.concatenate([hi, lo], axis=1), cum_ref[...],
                       preferred_element_type=F32)
        arg = z + sums[:, :tk] + jnp.concatenate([carry] * tiles_per_k, axis=1)
        a = jnp.exp(arg)
        if masked:
            a = jnp.where(mask, a, 0.0)
        acc = acc + jnp.dot(a.astype(BF16), v, preferred_element_type=F32)
        carry = carry + sums[:, tk:]
        return carry, acc

    def q_body(qi, _):
        qrows = pl.ds(pl.multiple_of(qi * tq, tq), tq)
        q = q_ref[qrows, :]
        kt_diag = qi // tiles_per_k
        zero = jnp.zeros((tq, HEAD_DIM), F32)
        carry, acc = tile(qi, kt_diag, q, zero, zero, True)

        def k_body(n, c):
            return tile(qi, kt_diag - 1 - n, q, c[0], c[1], False)

        carry, acc = lax.fori_loop(0, kt_diag, k_body, (carry, acc))
        o_ref[qrows, :] = acc.astype(o_ref.dtype)
        return 0

    lax.fori_loop(0, seq // tq, q_body, 0)


def _cumsum_matrix():
    tq, tk = ATTN_TQ, ATTN_TK
    j = np.arange(tk)[:, None]
    s = np.arange(tk)[None, :]
    incl = (j >= s).astype(np.float32)
    total = np.ones((tk, tq), np.float32)
    one = np.concatenate([incl, total], axis=1)
    return jnp.asarray(np.concatenate([one, one], axis=0), dtype=BF16)


def _sb_attention(qkv, batch, seq, heads):
    m = qkv.shape[0]
    dh = HEAD_DIM
    scale = float(1.0 / np.sqrt(np.float32(dh)))
    cum = _cumsum_matrix()
    return pl.pallas_call(
        functools.partial(_attn_kernel, scale=scale),
        grid=(batch, heads),
        in_specs=[
            pl.BlockSpec((seq, dh), lambda b, h: (b, h)),
            pl.BlockSpec((seq, dh), lambda b, h: (b, heads + h)),
            pl.BlockSpec((seq, dh), lambda b, h: (b, 2 * heads + h)),
            pl.BlockSpec(cum.shape, lambda b, h: (0, 0)),
        ],
        out_specs=pl.BlockSpec((seq, dh), lambda b, h: (b, h)),
        out_shape=jax.ShapeDtypeStruct((m, heads * dh), BF16),
        compiler_params=pltpu.CompilerParams(
            dimension_semantics=("arbitrary", "arbitrary"),
            vmem_limit_bytes=VMEM_LIMIT_BYTES),
        name="sb_attn",
    )(qkv, qkv, qkv, cum)


def _mix_kernel(x_ref, up_ref, halo_ref, ysb_ref, wout_ref, wpool_ref, pscale_ref,
                ling_ref, linb_ref, l1g_ref, l1b_ref, h1_ref, *, seq, alpha):
    i = pl.program_id(0)
    tm = x_ref.shape[0]
    gw = wpool_ref.shape[1]
    blk_in_seq = i % (seq // tm)
    halo = jnp.where(blk_in_seq == 0, 0.0, halo_ref[...])
    u = up_ref[...]
    ext = jnp.concatenate([halo, u], axis=0)
    t = blk_in_seq * tm + lax.broadcasted_iota(jnp.int32, (tm, 1), 0)
    parts = []
    for g, w in enumerate(POOL_WINDOWS):
        cols = slice(g * gw, (g + 1) * gw)
        s = ext[:, cols]
        d = 1
        while d < w:
            s = s + pltpu.roll(s, d, axis=0)
            d *= 2
        s = s[POOL_HALO:, :]
        count = jnp.minimum(t + 1, w).astype(F32)
        y = s / count - u[:, cols]
        yp = jnp.dot(y.astype(BF16), wpool_ref[g], preferred_element_type=F32)
        parts.append((yp * pscale_ref[:, cols]).astype(BF16))
    mix_in = jnp.concatenate(parts + [ysb_ref[...]], axis=1)
    mix = jnp.dot(mix_in, wout_ref[...], preferred_element_type=F32)
    h = _layer_norm(x_ref[...], ling_ref[...], linb_ref[...])
    h1_ref[...] = _layer_norm(alpha * h + mix, l1g_ref[...], l1b_ref[...])


def _mix_ln1(x2, u_pool, y_sb, w_out_bf16, w_pool_bf16, pool_scale, ln_in_g, ln_in_b,
             ln1_g, ln1_b, seq, alpha):
    m, d = x2.shape
    pw = u_pool.shape[1]
    sw = y_sb.shape[1]
    tm = MIX_TM
    assert seq % tm == 0 and tm % POOL_HALO == 0 and max(POOL_WINDOWS) <= POOL_HALO
    halo_blocks = tm // POOL_HALO
    const = lambda i: (0, 0)
    return pl.pallas_call(
        functools.partial(_mix_kernel, seq=seq, alpha=alpha),
        grid=(m // tm,),
        in_specs=[
            pl.BlockSpec((tm, d), lambda i: (i, 0)),
            pl.BlockSpec((tm, pw), lambda i: (i, 0)),
            pl.BlockSpec((POOL_HALO, pw), lambda i: (jnp.maximum(i * halo_blocks - 1, 0), 0)),
            pl.BlockSpec((tm, sw), lambda i: (i, 0)),
            pl.BlockSpec(w_out_bf16.shape, const, pipeline_mode=pl.Buffered(1)),
            pl.BlockSpec(w_pool_bf16.shape, lambda i: (0, 0, 0)),
            pl.BlockSpec((1, pw), const),
            pl.BlockSpec((1, d), const),
            pl.BlockSpec((1, d), const),
            pl.BlockSpec((1, d), const),
            pl.BlockSpec((1, d), const),
        ],
        out_specs=pl.BlockSpec((tm, d), lambda i: (i, 0)),
        out_shape=jax.ShapeDtypeStruct((m, d), F32),
        compiler_params=pltpu.CompilerParams(
            dimension_semantics=("arbitrary",),
            vmem_limit_bytes=VMEM_LIMIT_BYTES),
        name="mix_ln1",
    )(x2, u_pool, u_pool, y_sb, w_out_bf16, w_pool_bf16, pool_scale,
      ln_in_g, ln_in_b, ln1_g, ln1_b)


def _ffn_kernel(h1_ref, w1_ref, b1_ref, w2_ref, b2_ref, g_ref, b_ref, o_ref,
                h1b_scr, acc_scr, *, alpha):
    f = pl.program_id(1)

    @pl.when(f == 0)
    def _():
        h1b_scr[...] = h1_ref[...].astype(BF16)
        acc_scr[...] = jnp.zeros_like(acc_scr)

    t = jnp.dot(h1b_scr[...], w1_ref[...], preferred_element_type=F32) + b1_ref[...]
    t = jnp.maximum(t, 0.0)
    t = (t * t).astype(BF16)
    acc_scr[...] += jnp.dot(t, w2_ref[...], preferred_element_type=F32)

    @pl.when(f == pl.num_programs(1) - 1)
    def _():
        r = alpha * h1_ref[...] + (acc_scr[...] + b2_ref[...])
        o_ref[...] = _layer_norm(r, g_ref[...], b_ref[...])


def _ffn_ln2(h1, w1_bf16, b1, w2_bf16, b2, g, b, alpha):
    m, d = h1.shape
    dff = w1_bf16.shape[1]
    tm, tf = FFN_TM, FFN_TF
    assert m % tm == 0 and dff % tf == 0
    const = lambda i, f: (0, 0)
    return pl.pallas_call(
        functools.partial(_ffn_kernel, alpha=alpha),
        grid=(m // tm, dff // tf),
        in_specs=[
            pl.BlockSpec((tm, d), lambda i, f: (i, 0)),
            pl.BlockSpec((d, tf), lambda i, f: (0, f)),
            pl.BlockSpec((1, tf), lambda i, f: (0, f)),
            pl.BlockSpec((tf, d), lambda i, f: (f, 0)),
            pl.BlockSpec((1, d), const),
            pl.BlockSpec((1, d), const),
            pl.BlockSpec((1, d), const),
        ],
        out_specs=pl.BlockSpec((tm, d), lambda i, f: (i, 0)),
        out_shape=jax.ShapeDtypeStruct((m, d), F32),
        scratch_shapes=[pltpu.VMEM((tm, d), BF16), pltpu.VMEM((tm, d), F32)],
        compiler_params=pltpu.CompilerParams(
            dimension_semantics=("arbitrary", "arbitrary"),
            vmem_limit_bytes=VMEM_LIMIT_BYTES),
        name="ffn_ln2",
    )(h1, w1_bf16, b1, w2_bf16, b2, g, b)


def kernel(x, ln_in_g, ln_in_b, w_in, w_pool, pool_scale, w_out, ln1_g, ln1_b,
           w_ff1, b_ff1, w_ff2, b_ff2, ln2_g, ln2_b):
    batch, seq, d = x.shape
    depth, groups, gw, _ = w_pool.shape
    assert depth == 1, "single trunk layer"
    pool_width = groups * gw
    sb_width = (w_in.shape[2] - pool_width) // 3
    heads = sb_width // HEAD_DIM
    alpha = float((2.0 * depth) ** 0.25)
    m = batch * seq

    x2 = x.reshape(m, d)
    row = lambda p: p.reshape(1, -1)

    u_pool, qkv = _in_proj(x2, row(ln_in_g), row(ln_in_b), w_in[0].astype(BF16), pool_width)
    y_sb = _sb_attention(qkv, batch, seq, heads)
    h1 = _mix_ln1(x2, u_pool, y_sb, w_out[0].astype(BF16), w_pool[0].astype(BF16),
                  row(pool_scale[0]), row(ln_in_g), row(ln_in_b), row(ln1_g[0]), row(ln1_b[0]),
                  seq, alpha)
    out = _ffn_ln2(h1, w_ff1[0].astype(BF16), row(b_ff1[0]), w_ff2[0].astype(BF16),
                   row(b_ff2[0]), row(ln2_g[0]), row(ln2_b[0]), alpha)
    return out.reshape(batch, seq, d)
```

```python
import functools

import jax
import jax.numpy as jnp
import numpy as np
from jax import lax
from jax.experimental import pallas as pl
from jax.experimental.pallas import tpu as pltpu

F32 = jnp.float32
BF16 = jnp.bfloat16

LN_EPS = 1e-5
POOL_WINDOWS = (2, 4, 8, 16)
POOL_HALO = 16
HEAD_DIM = 128

VMEM_LIMIT_BYTES = 56 * 1024 * 1024

INPROJ_TM = 1024
INPROJ_TN = 1024
ATTN_T = 256
ATTN_HEADS = 4
MIX_TM = 512
FFN_TM = 512
FFN_TF = 1024
LN_ROWS = 128


def _layer_norm(x, g, b):
    mu = jnp.mean(x, axis=-1, keepdims=True)
    xc = x - mu
    var = jnp.mean(xc * xc, axis=-1, keepdims=True)
    return xc * lax.rsqrt(var + LN_EPS) * g + b


def _inproj_kernel(x_ref, g_ref, b_ref, w_ref, pool_ref, qkv_ref, h_scr):
    j = pl.program_id(1)
    tm = x_ref.shape[0]

    @pl.when(j == 0)
    def _():
        def body(r, carry):
            rows = pl.ds(pl.multiple_of(r * LN_ROWS, LN_ROWS), LN_ROWS)
            h_scr[rows, :] = _layer_norm(x_ref[rows, :], g_ref[...], b_ref[...]).astype(BF16)
            return carry
        lax.fori_loop(0, tm // LN_ROWS, body, 0)

    acc = jnp.dot(h_scr[...], w_ref[...], preferred_element_type=F32)

    @pl.when(j == 0)
    def _():
        pool_ref[...] = acc

    @pl.when(j > 0)
    def _():
        qkv_ref[...] = acc.astype(BF16)


def _in_proj(x2, g, b, w_in_bf16, pool_width):
    m, d = x2.shape
    n = w_in_bf16.shape[1]
    tm, tn = INPROJ_TM, INPROJ_TN
    assert pool_width == tn and m % tm == 0 and n % tn == 0
    return pl.pallas_call(
        _inproj_kernel,
        grid=(m // tm, n // tn),
        in_specs=[
            pl.BlockSpec((tm, d), lambda i, j: (i, 0)),
            pl.BlockSpec((1, d), lambda i, j: (0, 0)),
            pl.BlockSpec((1, d), lambda i, j: (0, 0)),
            pl.BlockSpec((d, tn), lambda i, j: (0, j)),
        ],
        out_specs=[
            pl.BlockSpec((tm, tn), lambda i, j: (i, 0)),
            pl.BlockSpec((tm, tn), lambda i, j: (i, jnp.maximum(j - 1, 0))),
        ],
        out_shape=[
            jax.ShapeDtypeStruct((m, pool_width), F32),
            jax.ShapeDtypeStruct((m, n - pool_width), BF16),
        ],
        scratch_shapes=[pltpu.VMEM((tm, d), BF16)],
        compiler_params=pltpu.CompilerParams(
            dimension_semantics=("arbitrary", "arbitrary"),
            vmem_limit_bytes=VMEM_LIMIT_BYTES),
        name="in_proj",
    )(x2, g, b, w_in_bf16)


def _attn_kernel(q_ref, k_ref, v_ref, cum_ref, o_ref, *, scale):
    seq = q_ref.shape[0]
    nh = q_ref.shape[1] // HEAD_DIM
    t = ATTN_T

    def tile(qi, kt, carry, acc, masked):
        qrows = pl.ds(pl.multiple_of(qi * t, t), t)
        krows = pl.ds(pl.multiple_of(kt * t, t), t)
        if masked:
            mask = (lax.broadcasted_iota(jnp.int32, (t, t), 1)
                    < lax.broadcasted_iota(jnp.int32, (t, t), 0))
        heads = range(nh)
        cols = [slice(h * HEAD_DIM, (h + 1) * HEAD_DIM) for h in heads]
        z = [lax.dot_general(q_ref[qrows, cols[h]], k_ref[krows, cols[h]],
                             (((1,), (1,)), ((), ())),
                             preferred_element_type=F32) * scale for h in heads]
        cs = []
        for h in heads:
            sp = jnp.maximum(z[h], 0.0) + jnp.log(1.0 + jnp.exp(-jnp.abs(z[h])))
            if masked:
                sp = jnp.where(mask, sp, 0.0)
            hi = sp.astype(BF16)
            lo = (sp - hi.astype(F32)).astype(BF16)
            cs.append(jnp.dot(jnp.concatenate([hi, lo], axis=1), cum_ref[...],
                              preferred_element_type=F32))
        new_carry, new_acc = [], []
        for h in heads:
            a = jnp.exp(z[h] - cs[h] - carry[h])
            if masked:
                a = jnp.where(mask, a, 0.0)
            new_acc.append(acc[h] + jnp.dot(a.astype(BF16), v_ref[krows, cols[h]],
                                            preferred_element_type=F32))
            new_carry.append(carry[h] + cs[h][:, 0:1])
        return new_carry, new_acc

    def q_body(qi, _):
        carry = [jnp.zeros((t, 1), F32)] * nh
        acc = [jnp.zeros((t, HEAD_DIM), F32)] * nh
        carry, acc = tile(qi, qi, carry, acc, True)

        def k_body(n, c):
            return tile(qi, qi - 1 - n, c[0], c[1], False)

        carry, acc = lax.fori_loop(0, qi, k_body, (carry, acc))
        qrows = pl.ds(pl.multiple_of(qi * t, t), t)
        for h in range(nh):
            o_ref[qrows, h * HEAD_DIM:(h + 1) * HEAD_DIM] = acc[h].astype(o_ref.dtype)
        return 0

    lax.fori_loop(0, seq // t, q_body, 0)


def _cumsum_matrix():
    j = np.arange(ATTN_T)[:, None]
    s = np.arange(ATTN_T)[None, :]
    incl = (j >= s).astype(np.float32)
    return jnp.asarray(np.concatenate([incl, incl], axis=0), dtype=BF16)


def _sb_attention(qkv, batch, seq, heads):
    m = qkv.shape[0]
    nh = ATTN_HEADS
    assert heads % nh == 0 and seq % ATTN_T == 0
    groups = heads // nh
    width = nh * HEAD_DIM
    scale = float(1.0 / np.sqrt(np.float32(HEAD_DIM)))
    cum = _cumsum_matrix()
    return pl.pallas_call(
        functools.partial(_attn_kernel, scale=scale),
        grid=(batch, groups),
        in_specs=[
            pl.BlockSpec((seq, width), lambda b, g: (b, g)),
            pl.BlockSpec((seq, width), lambda b, g: (b, groups + g)),
            pl.BlockSpec((seq, width), lambda b, g: (b, 2 * groups + g)),
            pl.BlockSpec(cum.shape, lambda b, g: (0, 0)),
        ],
        out_specs=pl.BlockSpec((seq, width), lambda b, g: (b, g)),
        out_shape=jax.ShapeDtypeStruct((m, heads * HEAD_DIM), BF16),
        compiler_params=pltpu.CompilerParams(
            dimension_semantics=("arbitrary", "arbitrary"),
            vmem_limit_bytes=VMEM_LIMIT_BYTES),
        name="sb_attn",
    )(qkv, qkv, qkv, cum)


def _mix_kernel(x_ref, up_ref, halo_ref, ysb_ref, wout_ref, wpool_ref, pscale_ref,
                ling_ref, linb_ref, l1g_ref, l1b_ref, h1_ref, *, seq, alpha):
    i = pl.program_id(0)
    tm = x_ref.shape[0]
    gw = wpool_ref.shape[1]
    blk_in_seq = i % (seq // tm)
    halo = jnp.where(blk_in_seq == 0, 0.0, halo_ref[...])
    u = up_ref[...]
    ext = jnp.concatenate([halo, u], axis=0)
    t = blk_in_seq * tm + lax.broadcasted_iota(jnp.int32, (tm, 1), 0)
    parts = []
    for g, w in enumerate(POOL_WINDOWS):
        cols = slice(g * gw, (g + 1) * gw)
        s = ext[:, cols]
        d = 1
        while d < w:
            s = s + pltpu.roll(s, d, axis=0)
            d *= 2
        s = s[POOL_HALO:, :]
        count = jnp.minimum(t + 1, w).astype(F32)
        y = s / count - u[:, cols]
        yp = jnp.dot(y.astype(BF16), wpool_ref[g], preferred_element_type=F32)
        parts.append((yp * pscale_ref[:, cols]).astype(BF16))
    mix_in = jnp.concatenate(parts + [ysb_ref[...]], axis=1)
    mix = jnp.dot(mix_in, wout_ref[...], preferred_element_type=F32)
    h = _layer_norm(x_ref[...], ling_ref[...], linb_ref[...])
    h1_ref[...] = _layer_norm(alpha * h + mix, l1g_ref[...], l1b_ref[...])


def _mix_ln1(x2, u_pool, y_sb, w_out_bf16, w_pool_bf16, pool_scale, ln_in_g, ln_in_b,
             ln1_g, ln1_b, seq, alpha):
    m, d = x2.shape
    pw = u_pool.shape[1]
    sw = y_sb.shape[1]
    tm = MIX_TM
    assert seq % tm == 0 and tm % POOL_HALO == 0 and max(POOL_WINDOWS) <= POOL_HALO
    halo_blocks = tm // POOL_HALO
    const = lambda i: (0, 0)
    return pl.pallas_call(
        functools.partial(_mix_kernel, seq=seq, alpha=alpha),
        grid=(m // tm,),
        in_specs=[
            pl.BlockSpec((tm, d), lambda i: (i, 0)),
            pl.BlockSpec((tm, pw), lambda i: (i, 0)),
            pl.BlockSpec((POOL_HALO, pw), lambda i: (jnp.maximum(i * halo_blocks - 1, 0), 0)),
            pl.BlockSpec((tm, sw), lambda i: (i, 0)),
            pl.BlockSpec(w_out_bf16.shape, const, pipeline_mode=pl.Buffered(1)),
            pl.BlockSpec(w_pool_bf16.shape, lambda i: (0, 0, 0)),
            pl.BlockSpec((1, pw), const),
            pl.BlockSpec((1, d), const),
            pl.BlockSpec((1, d), const),
            pl.BlockSpec((1, d), const),
            pl.BlockSpec((1, d), const),
        ],
        out_specs=pl.BlockSpec((tm, d), lambda i: (i, 0)),
        out_shape=jax.ShapeDtypeStruct((m, d), F32),
        compiler_params=pltpu.CompilerParams(
            dimension_semantics=("arbitrary",),
            vmem_limit_bytes=VMEM_LIMIT_BYTES),
        name="mix_ln1",
    )(x2, u_pool, u_pool, y_sb, w_out_bf16, w_pool_bf16, pool_scale,
      ln_in_g, ln_in_b, ln1_g, ln1_b)


def _ffn_kernel(h1_ref, w1_ref, b1_ref, w2_ref, b2_ref, g_ref, b_ref, o_ref,
                h1b_scr, acc_scr, *, alpha):
    f = pl.program_id(1)

    @pl.when(f == 0)
    def _():
        h1b_scr[...] = h1_ref[...].astype(BF16)
        acc_scr[...] = jnp.zeros_like(acc_scr)

    t = jnp.dot(h1b_scr[...], w1_ref[...], preferred_element_type=F32) + b1_ref[...]
    t = jnp.maximum(t, 0.0)
    t = (t * t).astype(BF16)
    acc_scr[...] += jnp.dot(t, w2_ref[...], preferred_element_type=F32)

    @pl.when(f == pl.num_programs(1) - 1)
    def _():
        r = alpha * h1_ref[...] + (acc_scr[...] + b2_ref[...])
        o_ref[...] = _layer_norm(r, g_ref[...], b_ref[...])


def _ffn_ln2(h1, w1_bf16, b1, w2_bf16, b2, g, b, alpha):
    m, d = h1.shape
    dff = w1_bf16.shape[1]
    tm, tf = FFN_TM, FFN_TF
    assert m % tm == 0 and dff % tf == 0
    const = lambda i, f: (0, 0)
    return pl.pallas_call(
        functools.partial(_ffn_kernel, alpha=alpha),
        grid=(m // tm, dff // tf),
        in_specs=[
            pl.BlockSpec((tm, d), lambda i, f: (i, 0)),
            pl.BlockSpec((d, tf), lambda i, f: (0, f)),
            pl.BlockSpec((1, tf), lambda i, f: (0, f)),
            pl.BlockSpec((tf, d), lambda i, f: (f, 0)),
            pl.BlockSpec((1, d), const),
            pl.BlockSpec((1, d), const),
            pl.BlockSpec((1, d), const),
        ],
        out_specs=pl.BlockSpec((tm, d), lambda i, f: (i, 0)),
        out_shape=jax.ShapeDtypeStruct((m, d), F32),
        scratch_shapes=[pltpu.VMEM((tm, d), BF16), pltpu.VMEM((tm, d), F32)],
        compiler_params=pltpu.CompilerParams(
            dimension_semantics=("arbitrary", "arbitrary"),
            vmem_limit_bytes=VMEM_LIMIT_BYTES),
        name="ffn_ln2",
    )(h1, w1_bf16, b1, w2_bf16, b2, g, b)


def kernel(x, ln_in_g, ln_in_b, w_in, w_pool, pool_scale, w_out, ln1_g, ln1_b,
           w_ff1, b_ff1, w_ff2, b_ff2, ln2_g, ln2_b):
    batch, seq, d = x.shape
    depth, groups, gw, _ = w_pool.shape
    assert depth == 1, "single trunk layer"
    pool_width = groups * gw
    sb_width = (w_in.shape[2] - pool_width) // 3
    heads = sb_width // HEAD_DIM
    alpha = float((2.0 * depth) ** 0.25)
    m = batch * seq

    x2 = x.reshape(m, d)
    row = lambda p: p.reshape(1, -1)

    u_pool, qkv = _in_proj(x2, row(ln_in_g), row(ln_in_b), w_in[0].astype(BF16), pool_width)
    y_sb = _sb_attention(qkv, batch, seq, heads)
    h1 = _mix_ln1(x2, u_pool, y_sb, w_out[0].astype(BF16), w_pool[0].astype(BF16),
                  row(pool_scale[0]), row(ln_in_g), row(ln_in_b), row(ln1_g[0]), row(ln1_b[0]),
                  seq, alpha)
    out = _ffn_ln2(h1, w_ff1[0].astype(BF16), row(b_ff1[0]), w_ff2[0].astype(BF16),
                   row(b_ff2[0]), row(ln2_g[0]), row(ln2_b[0]), alpha)
    return out.reshape(batch, seq, d)
```

```python
import functools

import jax
import jax.numpy as jnp
import numpy as np
from jax import lax
from jax.experimental import pallas as pl
from jax.experimental.pallas import tpu as pltpu

F32 = jnp.float32
BF16 = jnp.bfloat16

LN_EPS = 1e-5
LOG2_E = 1.4426950408889634
EXP2_CLAMP = 100.0
POOL_WINDOWS = (2, 4, 8, 16)
POOL_HALO = 16
HEAD_DIM = 128

VMEM_LIMIT_BYTES = 56 * 1024 * 1024

INPROJ_TM = 1024
INPROJ_TN = 1024
ATTN_T = 256
ATTN_HEADS = 8
ATTN_SKEW = 2
MIX_TM = 512
FFN_TM = 512
FFN_TF = 1024
LN_ROWS = 128


def _layer_norm(x, g, b):
    mu = jnp.mean(x, axis=-1, keepdims=True)
    xc = x - mu
    var = jnp.mean(xc * xc, axis=-1, keepdims=True)
    return xc * lax.rsqrt(var + LN_EPS) * g + b


def _inproj_kernel(x_ref, g_ref, b_ref, w_ref, pool_ref, qkv_ref, h_scr, *, q_scale):
    j = pl.program_id(1)
    tm = x_ref.shape[0]

    @pl.when(j == 0)
    def _():
        def body(r, carry):
            rows = pl.ds(pl.multiple_of(r * LN_ROWS, LN_ROWS), LN_ROWS)
            h_scr[rows, :] = _layer_norm(x_ref[rows, :], g_ref[...], b_ref[...]).astype(BF16)
            return carry
        lax.fori_loop(0, tm // LN_ROWS, body, 0)

    acc = jnp.dot(h_scr[...], w_ref[...], preferred_element_type=F32)

    @pl.when(j == 0)
    def _():
        pool_ref[...] = acc

    @pl.when(j == 1)
    def _():
        qkv_ref[...] = (acc * q_scale).astype(BF16)

    @pl.when(j > 1)
    def _():
        qkv_ref[...] = acc.astype(BF16)


def _in_proj(x2, g, b, w_in_bf16, pool_width, q_scale):
    m, d = x2.shape
    n = w_in_bf16.shape[1]
    tm, tn = INPROJ_TM, INPROJ_TN
    assert pool_width == tn and n == 4 * tn and m % tm == 0
    return pl.pallas_call(
        functools.partial(_inproj_kernel, q_scale=q_scale),
        grid=(m // tm, n // tn),
        in_specs=[
            pl.BlockSpec((tm, d), lambda i, j: (i, 0)),
            pl.BlockSpec((1, d), lambda i, j: (0, 0)),
            pl.BlockSpec((1, d), lambda i, j: (0, 0)),
            pl.BlockSpec((d, tn), lambda i, j: (0, j)),
        ],
        out_specs=[
            pl.BlockSpec((tm, tn), lambda i, j: (i, 0)),
            pl.BlockSpec((tm, tn), lambda i, j: (i, jnp.maximum(j - 1, 0))),
        ],
        out_shape=[
            jax.ShapeDtypeStruct((m, pool_width), F32),
            jax.ShapeDtypeStruct((m, n - pool_width), BF16),
        ],
        scratch_shapes=[pltpu.VMEM((tm, d), BF16)],
        compiler_params=pltpu.CompilerParams(
            dimension_semantics=("arbitrary", "arbitrary"),
            vmem_limit_bytes=VMEM_LIMIT_BYTES),
        name="in_proj",
    )(x2, g, b, w_in_bf16)


def _attn_kernel(q_ref, k_ref, v_ref, cum_ref, o_ref, carry_scr, acc_scr):
    qi = pl.program_id(2)
    t = ATTN_T
    nh = q_ref.shape[1] // HEAD_DIM
    heads = range(nh)
    cols = [slice(h * HEAD_DIM, (h + 1) * HEAD_DIM) for h in heads]

    def tile(kt, first):
        krows = pl.ds(pl.multiple_of(kt * t, t), t)
        if first:
            mask = (lax.broadcasted_iota(jnp.int32, (t, t), 1)
                    < lax.broadcasted_iota(jnp.int32, (t, t), 0))

        def scores(h):
            return lax.dot_general(q_ref[:, cols[h]], k_ref[krows, cols[h]],
                                   (((1,), (1,)), ((), ())),
                                   preferred_element_type=F32)

        def softplus_cumsum(w):
            sp = jnp.maximum(
                w, jnp.log(1.0 + jnp.exp2(jnp.minimum(w, EXP2_CLAMP))) * LOG2_E)
            if first:
                sp = jnp.where(mask, sp, 0.0)
            return jnp.dot(sp.astype(BF16), cum_ref[...], preferred_element_type=F32)

        def weights_values(h, w, cs):
            total = jnp.broadcast_to(cs[:, 0:1], (t, HEAD_DIM))
            if first:
                a = jnp.where(mask, jnp.exp2(w - cs), 0.0)
            else:
                carry = carry_scr[h]
                a = jnp.exp2(w - cs - jnp.concatenate([carry] * (t // HEAD_DIM), axis=1))
            av = jnp.dot(a.astype(BF16), v_ref[krows, cols[h]], preferred_element_type=F32)
            if first:
                carry_scr[h] = total
                acc_scr[h] = av
            else:
                carry_scr[h] = carry + total
                acc_scr[h] += av

        w, cs = {}, {}
        for h in range(min(ATTN_SKEW, nh)):
            w[h] = scores(h)
        for h in heads:
            if h + ATTN_SKEW < nh:
                w[h + ATTN_SKEW] = scores(h + ATTN_SKEW)
            cs[h] = softplus_cumsum(w[h])
            if h >= ATTN_SKEW:
                weights_values(h - ATTN_SKEW, w.pop(h - ATTN_SKEW), cs.pop(h - ATTN_SKEW))
        for h in range(max(nh - ATTN_SKEW, 0), nh):
            weights_values(h, w.pop(h), cs.pop(h))

    tile(qi, True)

    def k_body(n, _):
        tile(qi - 1 - n, False)
        return 0

    lax.fori_loop(0, qi, k_body, 0)
    for h in heads:
        o_ref[:, cols[h]] = acc_scr[h].astype(o_ref.dtype)


def _cumsum_matrix():
    j = np.arange(ATTN_T)[:, None]
    s = np.arange(ATTN_T)[None, :]
    return jnp.asarray(j >= s, dtype=BF16)


def _sb_attention(qkv, batch, seq, heads):
    m = qkv.shape[0]
    nh = ATTN_HEADS
    t = ATTN_T
    assert heads % nh == 0 and seq % t == 0
    groups = heads // nh
    nq = seq // t
    width = nh * HEAD_DIM
    cum = _cumsum_matrix()
    return pl.pallas_call(
        _attn_kernel,
        grid=(batch, groups, nq),
        in_specs=[
            pl.BlockSpec((t, width), lambda b, g, i: (b * nq + i, g)),
            pl.BlockSpec((seq, width), lambda b, g, i: (b, groups + g)),
            pl.BlockSpec((seq, width), lambda b, g, i: (b, 2 * groups + g)),
            pl.BlockSpec(cum.shape, lambda b, g, i: (0, 0)),
        ],
        out_specs=pl.BlockSpec((t, width), lambda b, g, i: (b * nq + i, g)),
        out_shape=jax.ShapeDtypeStruct((m, heads * HEAD_DIM), BF16),
        scratch_shapes=[pltpu.VMEM((nh, t, HEAD_DIM), F32), pltpu.VMEM((nh, t, HEAD_DIM), F32)],
        compiler_params=pltpu.CompilerParams(
            dimension_semantics=("arbitrary", "arbitrary", "arbitrary"),
            vmem_limit_bytes=VMEM_LIMIT_BYTES),
        name="sb_attn",
    )(qkv, qkv, qkv, cum)


def _mix_kernel(x_ref, up_ref, halo_ref, ysb_ref, wout_ref, wpool_ref, pscale_ref,
                ling_ref, linb_ref, l1g_ref, l1b_ref, h1_ref, *, seq, alpha):
    i = pl.program_id(0)
    tm = x_ref.shape[0]
    gw = wpool_ref.shape[1]
    blk_in_seq = i % (seq // tm)
    halo = jnp.where(blk_in_seq == 0, 0.0, halo_ref[...])
    u = up_ref[...]
    ext = jnp.concatenate([halo, u], axis=0)
    t = blk_in_seq * tm + lax.broadcasted_iota(jnp.int32, (tm, 1), 0)
    parts = []
    for g, w in enumerate(POOL_WINDOWS):
        cols = slice(g * gw, (g + 1) * gw)
        s = ext[:, cols]
        d = 1
        while d < w:
            s = s + pltpu.roll(s, d, axis=0)
            d *= 2
        s = s[POOL_HALO:, :]
        count = jnp.minimum(t + 1, w).astype(F32)
        y = s / count - u[:, cols]
        yp = jnp.dot(y.astype(BF16), wpool_ref[g], preferred_element_type=F32)
        parts.append((yp * pscale_ref[:, cols]).astype(BF16))
    mix_in = jnp.concatenate(parts + [ysb_ref[...]], axis=1)
    mix = jnp.dot(mix_in, wout_ref[...], preferred_element_type=F32)
    h = _layer_norm(x_ref[...], ling_ref[...], linb_ref[...])
    h1_ref[...] = _layer_norm(alpha * h + mix, l1g_ref[...], l1b_ref[...])


def _mix_ln1(x2, u_pool, y_sb, w_out_bf16, w_pool_bf16, pool_scale, ln_in_g, ln_in_b,
             ln1_g, ln1_b, seq, alpha):
    m, d = x2.shape
    pw = u_pool.shape[1]
    sw = y_sb.shape[1]
    tm = MIX_TM
    assert seq % tm == 0 and tm % POOL_HALO == 0 and max(POOL_WINDOWS) <= POOL_HALO
    halo_blocks = tm // POOL_HALO
    const = lambda i: (0, 0)
    return pl.pallas_call(
        functools.partial(_mix_kernel, seq=seq, alpha=alpha),
        grid=(m // tm,),
        in_specs=[
            pl.BlockSpec((tm, d), lambda i: (i, 0)),
            pl.BlockSpec((tm, pw), lambda i: (i, 0)),
            pl.BlockSpec((POOL_HALO, pw), lambda i: (jnp.maximum(i * halo_blocks - 1, 0), 0)),
            pl.BlockSpec((tm, sw), lambda i: (i, 0)),
            pl.BlockSpec(w_out_bf16.shape, const, pipeline_mode=pl.Buffered(1)),
            pl.BlockSpec(w_pool_bf16.shape, lambda i: (0, 0, 0)),
            pl.BlockSpec((1, pw), const),
            pl.BlockSpec((1, d), const),
            pl.BlockSpec((1, d), const),
            pl.BlockSpec((1, d), const),
            pl.BlockSpec((1, d), const),
        ],
        out_specs=pl.BlockSpec((tm, d), lambda i: (i, 0)),
        out_shape=jax.ShapeDtypeStruct((m, d), F32),
        compiler_params=pltpu.CompilerParams(
            dimension_semantics=("arbitrary",),
            vmem_limit_bytes=VMEM_LIMIT_BYTES),
        name="mix_ln1",
    )(x2, u_pool, u_pool, y_sb, w_out_bf16, w_pool_bf16, pool_scale,
      ln_in_g, ln_in_b, ln1_g, ln1_b)


def _ffn_kernel(h1_ref, w1_ref, b1_ref, w2_ref, b2_ref, g_ref, b_ref, o_ref,
                h1b_scr, acc_scr, *, alpha):
    f = pl.program_id(1)

    @pl.when(f == 0)
    def _():
        h1b_scr[...] = h1_ref[...].astype(BF16)
        acc_scr[...] = jnp.zeros_like(acc_scr)

    t = jnp.dot(h1b_scr[...], w1_ref[...], preferred_element_type=F32) + b1_ref[...]
    t = jnp.maximum(t, 0.0)
    t = (t * t).astype(BF16)
    acc_scr[...] += jnp.dot(t, w2_ref[...], preferred_element_type=F32)

    @pl.when(f == pl.num_programs(1) - 1)
    def _():
        r = alpha * h1_ref[...] + (acc_scr[...] + b2_ref[...])
        o_ref[...] = _layer_norm(r, g_ref[...], b_ref[...])


def _ffn_ln2(h1, w1_bf16, b1, w2_bf16, b2, g, b, alpha):
    m, d = h1.shape
    dff = w1_bf16.shape[1]
    tm, tf = FFN_TM, FFN_TF
    assert m % tm == 0 and dff % tf == 0
    const = lambda i, f: (0, 0)
    return pl.pallas_call(
        functools.partial(_ffn_kernel, alpha=alpha),
        grid=(m // tm, dff // tf),
        in_specs=[
            pl.BlockSpec((tm, d), lambda i, f: (i, 0)),
            pl.BlockSpec((d, tf), lambda i, f: (0, f)),
            pl.BlockSpec((1, tf), lambda i, f: (0, f)),
            pl.BlockSpec((tf, d), lambda i, f: (f, 0)),
            pl.BlockSpec((1, d), const),
            pl.BlockSpec((1, d), const),
            pl.BlockSpec((1, d), const),
        ],
        out_specs=pl.BlockSpec((tm, d), lambda i, f: (i, 0)),
        out_shape=jax.ShapeDtypeStruct((m, d), F32),
        scratch_shapes=[pltpu.VMEM((tm, d), BF16), pltpu.VMEM((tm, d), F32)],
        compiler_params=pltpu.CompilerParams(
            dimension_semantics=("arbitrary", "arbitrary"),
            vmem_limit_bytes=VMEM_LIMIT_BYTES),
        name="ffn_ln2",
    )(h1, w1_bf16, b1, w2_bf16, b2, g, b)


def kernel(x, ln_in_g, ln_in_b, w_in, w_pool, pool_scale, w_out, ln1_g, ln1_b,
           w_ff1, b_ff1, w_ff2, b_ff2, ln2_g, ln2_b):
    batch, seq, d = x.shape
    depth, groups, gw, _ = w_pool.shape
    assert depth == 1, "single trunk layer"
    pool_width = groups * gw
    sb_width = (w_in.shape[2] - pool_width) // 3
    heads = sb_width // HEAD_DIM
    alpha = float((2.0 * depth) ** 0.25)
    m = batch * seq

    x2 = x.reshape(m, d)
    row = lambda p: p.reshape(1, -1)

    q_scale = float(LOG2_E / np.sqrt(np.float32(HEAD_DIM)))
    u_pool, qkv = _in_proj(x2, row(ln_in_g), row(ln_in_b), w_in[0].astype(BF16), pool_width,
                           q_scale)
    y_sb = _sb_attention(qkv, batch, seq, heads)
    h1 = _mix_ln1(x2, u_pool, y_sb, w_out[0].astype(BF16), w_pool[0].astype(BF16),
                  row(pool_scale[0]), row(ln_in_g), row(ln_in_b), row(ln1_g[0]), row(ln1_b[0]),
                  seq, alpha)
    out = _ffn_ln2(h1, w_ff1[0].astype(BF16), row(b_ff1[0]), w_ff2[0].astype(BF16),
                   row(b_ff2[0]), row(ln2_g[0]), row(ln2_b[0]), alpha)
    return out.reshape(batch, seq, d)
```

```python
import functools

import jax
import jax.numpy as jnp
import numpy as np
from jax import lax
from jax.experimental import pallas as pl
from jax.experimental.pallas import tpu as pltpu

F32 = jnp.float32
BF16 = jnp.bfloat16

LN_EPS = 1e-5
LOG2_E = 1.4426950408889634
EXP2_CLAMP = 100.0
POOL_WINDOWS = (2, 4, 8, 16)
POOL_HALO = 16
HEAD_DIM = 128

VMEM_LIMIT_BYTES = 56 * 1024 * 1024

INPROJ_TM = 1024
INPROJ_TN = 1024
ATTN_T = 256
ATTN_HEADS = 8
ATTN_SKEW = 2
MIX_TM = 512
FFN_TM = 512
FFN_TF = 1024
LN_ROWS = 128


def _layer_norm(x, g, b):
    mu = jnp.mean(x, axis=-1, keepdims=True)
    xc = x - mu
    var = jnp.mean(xc * xc, axis=-1, keepdims=True)
    return xc * lax.rsqrt(var + LN_EPS) * g + b


def _inproj_kernel(x_ref, g_ref, b_ref, w_ref, pool_ref, qkv_ref, h_scr, *, q_scale):
    j = pl.program_id(1)
    tm = x_ref.shape[0]

    @pl.when(j == 0)
    def _():
        def body(r, carry):
            rows = pl.ds(pl.multiple_of(r * LN_ROWS, LN_ROWS), LN_ROWS)
            h_scr[rows, :] = _layer_norm(x_ref[rows, :], g_ref[...], b_ref[...]).astype(BF16)
            return carry
        lax.fori_loop(0, tm // LN_ROWS, body, 0)

    acc = jnp.dot(h_scr[...], w_ref[...].astype(BF16), preferred_element_type=F32)

    @pl.when(j == 0)
    def _():
        pool_ref[...] = acc

    @pl.when(j == 1)
    def _():
        qkv_ref[...] = (acc * q_scale).astype(BF16)

    @pl.when(j > 1)
    def _():
        qkv_ref[...] = acc.astype(BF16)


def _in_proj(x2, g, b, w_in, pool_width, q_scale):
    m, d = x2.shape
    n = w_in.shape[1]
    tm, tn = INPROJ_TM, INPROJ_TN
    assert pool_width == tn and n == 4 * tn and m % tm == 0
    return pl.pallas_call(
        functools.partial(_inproj_kernel, q_scale=q_scale),
        grid=(m // tm, n // tn),
        in_specs=[
            pl.BlockSpec((tm, d), lambda i, j: (i, 0)),
            pl.BlockSpec((1, d), lambda i, j: (0, 0)),
            pl.BlockSpec((1, d), lambda i, j: (0, 0)),
            pl.BlockSpec((d, tn), lambda i, j: (0, j)),
        ],
        out_specs=[
            pl.BlockSpec((tm, tn), lambda i, j: (i, 0)),
            pl.BlockSpec((tm, tn), lambda i, j: (i, jnp.maximum(j - 1, 0))),
        ],
        out_shape=[
            jax.ShapeDtypeStruct((m, pool_width), F32),
            jax.ShapeDtypeStruct((m, n - pool_width), BF16),
        ],
        scratch_shapes=[pltpu.VMEM((tm, d), BF16)],
        compiler_params=pltpu.CompilerParams(
            dimension_semantics=("arbitrary", "arbitrary"),
            vmem_limit_bytes=VMEM_LIMIT_BYTES),
        name="in_proj",
    )(x2, g, b, w_in)


def _attn_kernel(*refs, n_cast):
    q_ref, k_ref, v_ref, cum_ref = refs[:4]
    cast_in = refs[4:4 + n_cast]
    o_ref = refs[4 + n_cast]
    cast_out = refs[5 + n_cast:5 + 2 * n_cast]
    carry_scr, acc_scr = refs[5 + 2 * n_cast:]
    for src, dst in zip(cast_in, cast_out):
        dst[...] = src[...].astype(dst.dtype)
    qi = pl.program_id(2)
    t = ATTN_T
    nh = q_ref.shape[1] // HEAD_DIM
    heads = range(nh)
    cols = [slice(h * HEAD_DIM, (h + 1) * HEAD_DIM) for h in heads]

    def tile(kt, first):
        krows = pl.ds(pl.multiple_of(kt * t, t), t)
        if first:
            mask = (lax.broadcasted_iota(jnp.int32, (t, t), 1)
                    < lax.broadcasted_iota(jnp.int32, (t, t), 0))

        def scores(h):
            return lax.dot_general(q_ref[:, cols[h]], k_ref[krows, cols[h]],
                                   (((1,), (1,)), ((), ())),
                                   preferred_element_type=F32)

        def softplus_cumsum(w):
            sp = jnp.maximum(
                w, jnp.log(1.0 + jnp.exp2(jnp.minimum(w, EXP2_CLAMP))) * LOG2_E)
            if first:
                sp = jnp.where(mask, sp, 0.0)
            return jnp.dot(sp.astype(BF16), cum_ref[...], preferred_element_type=F32)

        def weights_values(h, w, cs):
            total = jnp.broadcast_to(cs[:, 0:1], (t, HEAD_DIM))
            if first:
                a = jnp.where(mask, jnp.exp2(w - cs), 0.0)
            else:
                carry = carry_scr[h]
                a = jnp.exp2(w - cs - jnp.concatenate([carry] * (t // HEAD_DIM), axis=1))
            av = jnp.dot(a.astype(BF16), v_ref[krows, cols[h]], preferred_element_type=F32)
            if first:
                carry_scr[h] = total
                acc_scr[h] = av
            else:
                carry_scr[h] = carry + total
                acc_scr[h] += av

        w, cs = {}, {}
        for h in range(min(ATTN_SKEW, nh)):
            w[h] = scores(h)
        for h in heads:
            if h + ATTN_SKEW < nh:
                w[h + ATTN_SKEW] = scores(h + ATTN_SKEW)
            cs[h] = softplus_cumsum(w[h])
            if h >= ATTN_SKEW:
                weights_values(h - ATTN_SKEW, w.pop(h - ATTN_SKEW), cs.pop(h - ATTN_SKEW))
        for h in range(max(nh - ATTN_SKEW, 0), nh):
            weights_values(h, w.pop(h), cs.pop(h))

    tile(qi, True)

    def k_body(n, _):
        tile(qi - 1 - n, False)
        return 0

    lax.fori_loop(0, qi, k_body, 0)
    for h in heads:
        o_ref[:, cols[h]] = acc_scr[h].astype(o_ref.dtype)


def _cumsum_matrix():
    j = np.arange(ATTN_T)[:, None]
    s = np.arange(ATTN_T)[None, :]
    return jnp.asarray(j >= s, dtype=BF16)


def _sb_attention(qkv, batch, seq, heads, weights_f32):
    m = qkv.shape[0]
    nh = ATTN_HEADS
    t = ATTN_T
    assert heads % nh == 0 and seq % t == 0
    groups = heads // nh
    nq = seq // t
    width = nh * HEAD_DIM
    cum = _cumsum_matrix()
    steps = batch * groups * nq
    step = lambda b, g, i: ((b * groups + g) * nq + i, 0)
    slabs = []
    for wgt in weights_f32:
        rows, ncol = wgt.shape
        assert rows % (steps * 16) == 0
        slabs.append(pl.BlockSpec((rows // steps, ncol), step))
    out = pl.pallas_call(
        functools.partial(_attn_kernel, n_cast=len(weights_f32)),
        grid=(batch, groups, nq),
        in_specs=[
            pl.BlockSpec((t, width), lambda b, g, i: (b * nq + i, g)),
            pl.BlockSpec((seq, width), lambda b, g, i: (b, groups + g)),
            pl.BlockSpec((seq, width), lambda b, g, i: (b, 2 * groups + g)),
            pl.BlockSpec(cum.shape, lambda b, g, i: (0, 0)),
        ] + slabs,
        out_specs=[pl.BlockSpec((t, width), lambda b, g, i: (b * nq + i, g))] + slabs,
        out_shape=[jax.ShapeDtypeStruct((m, heads * HEAD_DIM), BF16)]
        + [jax.ShapeDtypeStruct(wgt.shape, BF16) for wgt in weights_f32],
        scratch_shapes=[pltpu.VMEM((nh, t, HEAD_DIM), F32), pltpu.VMEM((nh, t, HEAD_DIM), F32)],
        compiler_params=pltpu.CompilerParams(
            dimension_semantics=("arbitrary", "arbitrary", "arbitrary"),
            vmem_limit_bytes=VMEM_LIMIT_BYTES),
        name="sb_attn",
    )(qkv, qkv, qkv, cum, *weights_f32)
    return out[0], out[1:]


def _mix_kernel(x_ref, up_ref, halo_ref, ysb_ref, wout_ref, wpool_ref, pscale_ref,
                ling_ref, linb_ref, l1g_ref, l1b_ref, h1_ref, *, seq, alpha):
    i = pl.program_id(0)
    tm = x_ref.shape[0]
    gw = wpool_ref.shape[1]
    blk_in_seq = i % (seq // tm)
    halo = jnp.where(blk_in_seq == 0, 0.0, halo_ref[...])
    u = up_ref[...]
    ext = jnp.concatenate([halo, u], axis=0)
    t = blk_in_seq * tm + lax.broadcasted_iota(jnp.int32, (tm, 1), 0)
    parts = []
    for g, w in enumerate(POOL_WINDOWS):
        cols = slice(g * gw, (g + 1) * gw)
        s = ext[:, cols]
        d = 1
        while d < w:
            s = s + pltpu.roll(s, d, axis=0)
            d *= 2
        s = s[POOL_HALO:, :]
        count = jnp.minimum(t + 1, w).astype(F32)
        y = s / count - u[:, cols]
        yp = jnp.dot(y.astype(BF16), wpool_ref[g], preferred_element_type=F32)
        parts.append((yp * pscale_ref[:, cols]).astype(BF16))
    mix_in = jnp.concatenate(parts + [ysb_ref[...]], axis=1)
    mix = jnp.dot(mix_in, wout_ref[...], preferred_element_type=F32)
    h = _layer_norm(x_ref[...], ling_ref[...], linb_ref[...])
    h1_ref[...] = _layer_norm(alpha * h + mix, l1g_ref[...], l1b_ref[...])


def _mix_ln1(x2, u_pool, y_sb, w_out_bf16, w_pool_bf16, pool_scale, ln_in_g, ln_in_b,
             ln1_g, ln1_b, seq, alpha):
    m, d = x2.shape
    pw = u_pool.shape[1]
    sw = y_sb.shape[1]
    tm = MIX_TM
    assert seq % tm == 0 and tm % POOL_HALO == 0 and max(POOL_WINDOWS) <= POOL_HALO
    halo_blocks = tm // POOL_HALO
    const = lambda i: (0, 0)
    return pl.pallas_call(
        functools.partial(_mix_kernel, seq=seq, alpha=alpha),
        grid=(m // tm,),
        in_specs=[
            pl.BlockSpec((tm, d), lambda i: (i, 0)),
            pl.BlockSpec((tm, pw), lambda i: (i, 0)),
            pl.BlockSpec((POOL_HALO, pw), lambda i: (jnp.maximum(i * halo_blocks - 1, 0), 0)),
            pl.BlockSpec((tm, sw), lambda i: (i, 0)),
            pl.BlockSpec(w_out_bf16.shape, const, pipeline_mode=pl.Buffered(1)),
            pl.BlockSpec(w_pool_bf16.shape, lambda i: (0, 0, 0)),
            pl.BlockSpec((1, pw), const),
            pl.BlockSpec((1, d), const),
            pl.BlockSpec((1, d), const),
            pl.BlockSpec((1, d), const),
            pl.BlockSpec((1, d), const),
        ],
        out_specs=pl.BlockSpec((tm, d), lambda i: (i, 0)),
        out_shape=jax.ShapeDtypeStruct((m, d), F32),
        compiler_params=pltpu.CompilerParams(
            dimension_semantics=("arbitrary",),
            vmem_limit_bytes=VMEM_LIMIT_BYTES),
        name="mix_ln1",
    )(x2, u_pool, u_pool, y_sb, w_out_bf16, w_pool_bf16, pool_scale,
      ln_in_g, ln_in_b, ln1_g, ln1_b)


def _ffn_kernel(h1_ref, w1_ref, b1_ref, w2_ref, b2_ref, g_ref, b_ref, o_ref,
                h1b_scr, acc_scr, *, alpha):
    f = pl.program_id(1)

    @pl.when(f == 0)
    def _():
        h1b_scr[...] = h1_ref[...].astype(BF16)
        acc_scr[...] = jnp.zeros_like(acc_scr)

    t = jnp.dot(h1b_scr[...], w1_ref[...], preferred_element_type=F32) + b1_ref[...]
    t = jnp.maximum(t, 0.0)
    t = (t * t).astype(BF16)
    acc_scr[...] += jnp.dot(t, w2_ref[...], preferred_element_type=F32)

    @pl.when(f == pl.num_programs(1) - 1)
    def _():
        r = alpha * h1_ref[...] + (acc_scr[...] + b2_ref[...])
        o_ref[...] = _layer_norm(r, g_ref[...], b_ref[...])


def _ffn_ln2(h1, w1_bf16, b1, w2_bf16, b2, g, b, alpha):
    m, d = h1.shape
    dff = w1_bf16.shape[1]
    tm, tf = FFN_TM, FFN_TF
    assert m % tm == 0 and dff % tf == 0
    const = lambda i, f: (0, 0)
    return pl.pallas_call(
        functools.partial(_ffn_kernel, alpha=alpha),
        grid=(m // tm, dff // tf),
        in_specs=[
            pl.BlockSpec((tm, d), lambda i, f: (i, 0)),
            pl.BlockSpec((d, tf), lambda i, f: (0, f)),
            pl.BlockSpec((1, tf), lambda i, f: (0, f)),
            pl.BlockSpec((tf, d), lambda i, f: (f, 0)),
            pl.BlockSpec((1, d), const),
            pl.BlockSpec((1, d), const),
            pl.BlockSpec((1, d), const),
        ],
        out_specs=pl.BlockSpec((tm, d), lambda i, f: (i, 0)),
        out_shape=jax.ShapeDtypeStruct((m, d), F32),
        scratch_shapes=[pltpu.VMEM((tm, d), BF16), pltpu.VMEM((tm, d), F32)],
        compiler_params=pltpu.CompilerParams(
            dimension_semantics=("arbitrary", "arbitrary"),
            vmem_limit_bytes=VMEM_LIMIT_BYTES),
        name="ffn_ln2",
    )(h1, w1_bf16, b1, w2_bf16, b2, g, b)


def kernel(x, ln_in_g, ln_in_b, w_in, w_pool, pool_scale, w_out, ln1_g, ln1_b,
           w_ff1, b_ff1, w_ff2, b_ff2, ln2_g, ln2_b):
    batch, seq, d = x.shape
    depth, groups, gw, _ = w_pool.shape
    assert depth == 1, "single trunk layer"
    pool_width = groups * gw
    sb_width = (w_in.shape[2] - pool_width) // 3
    heads = sb_width // HEAD_DIM
    alpha = float((2.0 * depth) ** 0.25)
    m = batch * seq

    x2 = x.reshape(m, d)
    row = lambda p: p.reshape(1, -1)

    q_scale = float(LOG2_E / np.sqrt(np.float32(HEAD_DIM)))
    u_pool, qkv = _in_proj(x2, row(ln_in_g), row(ln_in_b), w_in[0], pool_width, q_scale)
    y_sb, (w_out_bf16, w_ff1_bf16, w_ff2_bf16) = _sb_attention(
        qkv, batch, seq, heads, [w_out[0], w_ff1[0], w_ff2[0]])
    h1 = _mix_ln1(x2, u_pool, y_sb, w_out_bf16, w_pool[0].astype(BF16),
                  row(pool_scale[0]), row(ln_in_g), row(ln_in_b), row(ln1_g[0]), row(ln1_b[0]),
                  seq, alpha)
    out = _ffn_ln2(h1, w_ff1_bf16, row(b_ff1[0]), w_ff2_bf16,
                   row(b_ff2[0]), row(ln2_g[0]), row(ln2_b[0]), alpha)
    return out.reshape(batch, seq, d)
```

```python
import functools

import jax
import jax.numpy as jnp
import numpy as np
from jax import lax
from jax.experimental import pallas as pl
from jax.experimental.pallas import tpu as pltpu

F32 = jnp.float32
BF16 = jnp.bfloat16

LN_EPS = 1e-5
LOG2_E = 1.4426950408889634
EXP2_CLAMP = 100.0
POOL_WINDOWS = (2, 4, 8, 16)
POOL_HALO = 16
HEAD_DIM = 128

VMEM_LIMIT_BYTES = 56 * 1024 * 1024

INPROJ_TM = 1024
INPROJ_TN = 1024
ATTN_T = 256
ATTN_HEADS = 8
ATTN_SKEW = 2
ATTN_LAG = 4
MIX_TM = 512
FFN_TM = 512
FFN_TF = 1024
LN_ROWS = 128


def _layer_norm(x, g, b):
    mu = jnp.mean(x, axis=-1, keepdims=True)
    xc = x - mu
    var = jnp.mean(xc * xc, axis=-1, keepdims=True)
    return xc * lax.rsqrt(var + LN_EPS) * g + b


def _inproj_kernel(x_ref, g_ref, b_ref, w_ref, pool_ref, qkv_ref, h_scr, *, q_scale):
    j = pl.program_id(1)
    tm = x_ref.shape[0]

    @pl.when(j == 0)
    def _():
        def body(r, carry):
            rows = pl.ds(pl.multiple_of(r * LN_ROWS, LN_ROWS), LN_ROWS)
            h_scr[rows, :] = _layer_norm(x_ref[rows, :], g_ref[...], b_ref[...]).astype(BF16)
            return carry
        lax.fori_loop(0, tm // LN_ROWS, body, 0)

    acc = jnp.dot(h_scr[...], w_ref[...].astype(BF16), preferred_element_type=F32)

    @pl.when(j == 0)
    def _():
        pool_ref[...] = acc

    @pl.when(j == 1)
    def _():
        qkv_ref[...] = (acc * q_scale).astype(BF16)

    @pl.when(j > 1)
    def _():
        qkv_ref[...] = acc.astype(BF16)


def _in_proj(x2, g, b, w_in, pool_width, q_scale):
    m, d = x2.shape
    n = w_in.shape[1]
    tm, tn = INPROJ_TM, INPROJ_TN
    assert pool_width == tn and n == 4 * tn and m % tm == 0
    return pl.pallas_call(
        functools.partial(_inproj_kernel, q_scale=q_scale),
        grid=(m // tm, n // tn),
        in_specs=[
            pl.BlockSpec((tm, d), lambda i, j: (i, 0)),
            pl.BlockSpec((1, d), lambda i, j: (0, 0)),
            pl.BlockSpec((1, d), lambda i, j: (0, 0)),
            pl.BlockSpec((d, tn), lambda i, j: (0, j)),
        ],
        out_specs=[
            pl.BlockSpec((tm, tn), lambda i, j: (i, 0)),
            pl.BlockSpec((tm, tn), lambda i, j: (i, jnp.maximum(j - 1, 0))),
        ],
        out_shape=[
            jax.ShapeDtypeStruct((m, pool_width), F32),
            jax.ShapeDtypeStruct((m, n - pool_width), BF16),
        ],
        scratch_shapes=[pltpu.VMEM((tm, d), BF16)],
        compiler_params=pltpu.CompilerParams(
            dimension_semantics=("arbitrary", "arbitrary"),
            vmem_limit_bytes=VMEM_LIMIT_BYTES),
        name="in_proj",
    )(x2, g, b, w_in)


def _attn_kernel(*refs, n_cast):
    q_ref, k_ref, v_ref, cum_ref = refs[:4]
    cast_in = refs[4:4 + n_cast]
    o_ref = refs[4 + n_cast]
    cast_out = refs[5 + n_cast:5 + 2 * n_cast]
    carry_scr, acc_scr, w_scr, a_scr = refs[5 + 2 * n_cast:]
    for src, dst in zip(cast_in, cast_out):
        dst[...] = src[...].astype(dst.dtype)
    qi = pl.program_id(2)
    t = ATTN_T
    nh = q_ref.shape[1] // HEAD_DIM
    skew = ATTN_SKEW
    heads = range(nh)
    cols = [slice(h * HEAD_DIM, (h + 1) * HEAD_DIM) for h in heads]

    def key_rows(kt):
        return pl.ds(pl.multiple_of(kt * t, t), t)

    def scores(kt, h):
        return lax.dot_general(q_ref[:, cols[h]], k_ref[key_rows(kt), cols[h]],
                               (((1,), (1,)), ((), ())),
                               preferred_element_type=F32)

    def deferred_values(kt, j):
        h = nh - skew + j
        acc_scr[h] += jnp.dot(a_scr[j], v_ref[key_rows(kt), cols[h]],
                              preferred_element_type=F32)

    def tile(kt, first):
        krows = key_rows(kt)
        if first:
            mask = (lax.broadcasted_iota(jnp.int32, (t, t), 1)
                    < lax.broadcasted_iota(jnp.int32, (t, t), 0))

        def softplus_cumsum(w):
            sp = jnp.maximum(
                w, jnp.log(1.0 + jnp.exp2(jnp.minimum(w, EXP2_CLAMP))) * LOG2_E)
            if first:
                sp = jnp.where(mask, sp, 0.0)
            return jnp.dot(sp.astype(BF16), cum_ref[...], preferred_element_type=F32)

        def weights_values(h, w, cs):
            total = jnp.broadcast_to(cs[:, 0:1], (t, HEAD_DIM))
            if first:
                a = jnp.where(mask, jnp.exp2(w - cs), 0.0)
            else:
                carry = carry_scr[h]
                a = jnp.exp2(w - cs - jnp.concatenate([carry] * (t // HEAD_DIM), axis=1))
            carry_scr[h] = total if first else carry + total
            if h >= nh - skew:
                a_scr[h - (nh - skew)] = a.astype(BF16)
                if first:
                    acc_scr[h] = jnp.zeros((t, HEAD_DIM), F32)
                return
            av = jnp.dot(a.astype(BF16), v_ref[krows, cols[h]], preferred_element_type=F32)
            if first:
                acc_scr[h] = av
            else:
                acc_scr[h] += av

        kt_next = jnp.maximum(kt - 1, 0)
        w, cs, w_next = {}, {}, {}
        for j in range(skew):
            w[j] = scores(kt, j) if first else w_scr[j]
        for h in heads:
            if h + skew < nh:
                w[h + skew] = scores(kt, h + skew)
            else:
                w_next[h + skew - nh] = scores(kt_next, h + skew - nh)
            cs[h] = softplus_cumsum(w[h])
            if h >= ATTN_LAG:
                weights_values(h - ATTN_LAG, w.pop(h - ATTN_LAG), cs.pop(h - ATTN_LAG))
            if h < skew and not first:
                deferred_values(kt + 1, h)
        for h in range(nh - ATTN_LAG, nh):
            weights_values(h, w.pop(h), cs.pop(h))
        for j in range(skew):
            w_scr[j] = w_next[j]

    tile(qi, True)

    def k_body(n, _):
        tile(qi - 1 - n, False)
        return 0

    lax.fori_loop(0, qi, k_body, 0)
    for j in range(skew):
        deferred_values(0, j)
    for h in heads:
        o_ref[:, cols[h]] = acc_scr[h].astype(o_ref.dtype)


def _cumsum_matrix():
    j = np.arange(ATTN_T)[:, None]
    s = np.arange(ATTN_T)[None, :]
    return jnp.asarray(j >= s, dtype=BF16)


def _sb_attention(qkv, batch, seq, heads, weights_f32):
    m = qkv.shape[0]
    nh = ATTN_HEADS
    t = ATTN_T
    assert heads % nh == 0 and seq % t == 0
    groups = heads // nh
    nq = seq // t
    width = nh * HEAD_DIM
    cum = _cumsum_matrix()
    steps = batch * groups * nq
    step = lambda b, g, i: ((b * groups + g) * nq + i, 0)
    slabs = []
    for wgt in weights_f32:
        rows, ncol = wgt.shape
        assert rows % (steps * 16) == 0
        slabs.append(pl.BlockSpec((rows // steps, ncol), step))
    out = pl.pallas_call(
        functools.partial(_attn_kernel, n_cast=len(weights_f32)),
        grid=(batch, groups, nq),
        in_specs=[
            pl.BlockSpec((t, width), lambda b, g, i: (b * nq + i, g)),
            pl.BlockSpec((seq, width), lambda b, g, i: (b, groups + g)),
            pl.BlockSpec((seq, width), lambda b, g, i: (b, 2 * groups + g)),
            pl.BlockSpec(cum.shape, lambda b, g, i: (0, 0)),
        ] + slabs,
        out_specs=[pl.BlockSpec((t, width), lambda b, g, i: (b * nq + i, g))] + slabs,
        out_shape=[jax.ShapeDtypeStruct((m, heads * HEAD_DIM), BF16)]
        + [jax.ShapeDtypeStruct(wgt.shape, BF16) for wgt in weights_f32],
        scratch_shapes=[pltpu.VMEM((nh, t, HEAD_DIM), F32), pltpu.VMEM((nh, t, HEAD_DIM), F32),
                        pltpu.VMEM((ATTN_SKEW, t, t), F32), pltpu.VMEM((ATTN_SKEW, t, t), BF16)],
        compiler_params=pltpu.CompilerParams(
            dimension_semantics=("arbitrary", "arbitrary", "arbitrary"),
            vmem_limit_bytes=VMEM_LIMIT_BYTES),
        name="sb_attn",
    )(qkv, qkv, qkv, cum, *weights_f32)
    return out[0], out[1:]


def _mix_kernel(x_ref, up_ref, halo_ref, ysb_ref, wout_ref, wpool_ref, pscale_ref,
                ling_ref, linb_ref, l1g_ref, l1b_ref, h1_ref, *, seq, alpha):
    i = pl.program_id(0)
    tm = x_ref.shape[0]
    gw = wpool_ref.shape[1]
    blk_in_seq = i % (seq // tm)
    halo = jnp.where(blk_in_seq == 0, 0.0, halo_ref[...])
    u = up_ref[...]
    ext = jnp.concatenate([halo, u], axis=0)
    t = blk_in_seq * tm + lax.broadcasted_iota(jnp.int32, (tm, 1), 0)
    parts = []
    for g, w in enumerate(POOL_WINDOWS):
        cols = slice(g * gw, (g + 1) * gw)
        s = ext[:, cols]
        d = 1
        while d < w:
            s = s + pltpu.roll(s, d, axis=0)
            d *= 2
        s = s[POOL_HALO:, :]
        count = jnp.minimum(t + 1, w).astype(F32)
        y = s / count - u[:, cols]
        yp = jnp.dot(y.astype(BF16), wpool_ref[g], preferred_element_type=F32)
        parts.append((yp * pscale_ref[:, cols]).astype(BF16))
    mix_in = jnp.concatenate(parts + [ysb_ref[...]], axis=1)
    mix = jnp.dot(mix_in, wout_ref[...], preferred_element_type=F32)
    h = _layer_norm(x_ref[...], ling_ref[...], linb_ref[...])
    h1_ref[...] = _layer_norm(alpha * h + mix, l1g_ref[...], l1b_ref[...])


def _mix_ln1(x2, u_pool, y_sb, w_out_bf16, w_pool_bf16, pool_scale, ln_in_g, ln_in_b,
             ln1_g, ln1_b, seq, alpha):
    m, d = x2.shape
    pw = u_pool.shape[1]
    sw = y_sb.shape[1]
    tm = MIX_TM
    assert seq % tm == 0 and tm % POOL_HALO == 0 and max(POOL_WINDOWS) <= POOL_HALO
    halo_blocks = tm // POOL_HALO
    const = lambda i: (0, 0)
    return pl.pallas_call(
        functools.partial(_mix_kernel, seq=seq, alpha=alpha),
        grid=(m // tm,),
        in_specs=[
            pl.BlockSpec((tm, d), lambda i: (i, 0)),
            pl.BlockSpec((tm, pw), lambda i: (i, 0)),
            pl.BlockSpec((POOL_HALO, pw), lambda i: (jnp.maximum(i * halo_blocks - 1, 0), 0)),
            pl.BlockSpec((tm, sw), lambda i: (i, 0)),
            pl.BlockSpec(w_out_bf16.shape, const, pipeline_mode=pl.Buffered(1)),
            pl.BlockSpec(w_pool_bf16.shape, lambda i: (0, 0, 0)),
            pl.BlockSpec((1, pw), const),
            pl.BlockSpec((1, d), const),
            pl.BlockSpec((1, d), const),
            pl.BlockSpec((1, d), const),
            pl.BlockSpec((1, d), const),
        ],
        out_specs=pl.BlockSpec((tm, d), lambda i: (i, 0)),
        out_shape=jax.ShapeDtypeStruct((m, d), F32),
        compiler_params=pltpu.CompilerParams(
            dimension_semantics=("arbitrary",),
            vmem_limit_bytes=VMEM_LIMIT_BYTES),
        name="mix_ln1",
    )(x2, u_pool, u_pool, y_sb, w_out_bf16, w_pool_bf16, pool_scale,
      ln_in_g, ln_in_b, ln1_g, ln1_b)


def _ffn_kernel(h1_ref, w1_ref, b1_ref, w2_ref, b2_ref, g_ref, b_ref, o_ref,
                h1b_scr, acc_scr, *, alpha):
    f = pl.program_id(1)

    @pl.when(f == 0)
    def _():
        h1b_scr[...] = h1_ref[...].astype(BF16)
        acc_scr[...] = jnp.zeros_like(acc_scr)

    t = jnp.dot(h1b_scr[...], w1_ref[...], preferred_element_type=F32) + b1_ref[...]
    t = jnp.maximum(t, 0.0)
    t = (t * t).astype(BF16)
    acc_scr[...] += jnp.dot(t, w2_ref[...], preferred_element_type=F32)

    @pl.when(f == pl.num_programs(1) - 1)
    def _():
        r = alpha * h1_ref[...] + (acc_scr[...] + b2_ref[...])
        o_ref[...] = _layer_norm(r, g_ref[...], b_ref[...])


def _ffn_ln2(h1, w1_bf16, b1, w2_bf16, b2, g, b, alpha):
    m, d = h1.shape
    dff = w1_bf16.shape[1]
    tm, tf = FFN_TM, FFN_TF
    assert m % tm == 0 and dff % tf == 0
    const = lambda i, f: (0, 0)
    return pl.pallas_call(
        functools.partial(_ffn_kernel, alpha=alpha),
        grid=(m // tm, dff // tf),
        in_specs=[
            pl.BlockSpec((tm, d), lambda i, f: (i, 0)),
            pl.BlockSpec((d, tf), lambda i, f: (0, f)),
            pl.BlockSpec((1, tf), lambda i, f: (0, f)),
            pl.BlockSpec((tf, d), lambda i, f: (f, 0)),
            pl.BlockSpec((1, d), const),
            pl.BlockSpec((1, d), const),
            pl.BlockSpec((1, d), const),
        ],
        out_specs=pl.BlockSpec((tm, d), lambda i, f: (i, 0)),
        out_shape=jax.ShapeDtypeStruct((m, d), F32),
        scratch_shapes=[pltpu.VMEM((tm, d), BF16), pltpu.VMEM((tm, d), F32)],
        compiler_params=pltpu.CompilerParams(
            dimension_semantics=("arbitrary", "arbitrary"),
            vmem_limit_bytes=VMEM_LIMIT_BYTES),
        name="ffn_ln2",
    )(h1, w1_bf16, b1, w2_bf16, b2, g, b)


def kernel(x, ln_in_g, ln_in_b, w_in, w_pool, pool_scale, w_out, ln1_g, ln1_b,
           w_ff1, b_ff1, w_ff2, b_ff2, ln2_g, ln2_b):
    batch, seq, d = x.shape
    depth, groups, gw, _ = w_pool.shape
    assert depth == 1, "single trunk layer"
    pool_width = groups * gw
    sb_width = (w_in.shape[2] - pool_width) // 3
    heads = sb_width // HEAD_DIM
    alpha = float((2.0 * depth) ** 0.25)
    m = batch * seq

    x2 = x.reshape(m, d)
    row = lambda p: p.reshape(1, -1)

    q_scale = float(LOG2_E / np.sqrt(np.float32(HEAD_DIM)))
    u_pool, qkv = _in_proj(x2, row(ln_in_g), row(ln_in_b), w_in[0], pool_width, q_scale)
    y_sb, (w_out_bf16, w_ff1_bf16, w_ff2_bf16) = _sb_attention(
        qkv, batch, seq, heads, [w_out[0], w_ff1[0], w_ff2[0]])
    h1 = _mix_ln1(x2, u_pool, y_sb, w_out_bf16, w_pool[0].astype(BF16),
                  row(pool_scale[0]), row(ln_in_g), row(ln_in_b), row(ln1_g[0]), row(ln1_b[0]),
                  seq, alpha)
    out = _ffn_ln2(h1, w_ff1_bf16, row(b_ff1[0]), w_ff2_bf16,
                   row(b_ff2[0]), row(ln2_g[0]), row(ln2_b[0]), alpha)
    return out.reshape(batch, seq, d)
```

```python
import functools

import jax
import jax.numpy as jnp
import numpy as np
from jax import lax
from jax.experimental import pallas as pl
from jax.experimental.pallas import tpu as pltpu

F32 = jnp.float32
BF16 = jnp.bfloat16

LN_EPS = 1e-5
LOG2_E = 1.4426950408889634
EXP2_CLAMP = 100.0
POOL_WINDOWS = (2, 4, 8, 16)
POOL_HALO = 16
HEAD_DIM = 128

VMEM_LIMIT_BYTES = 56 * 1024 * 1024

INPROJ_TM = 1024
INPROJ_TN = 1024
ATTN_T = 256
ATTN_HEADS = 8
ATTN_SKEW = 2
ATTN_LAG = 4
MIX_TM = 512
FFN_TM = 512
FFN_TF = 1024
LN_ROWS = 128


def _layer_norm(x, g, b):
    mu = jnp.mean(x, axis=-1, keepdims=True)
    xc = x - mu
    var = jnp.mean(xc * xc, axis=-1, keepdims=True)
    return xc * lax.rsqrt(var + LN_EPS) * g + b


def _inproj_kernel(x_ref, g_ref, b_ref, w_ref, pool_ref, qkv_ref, h_scr, *, q_scale):
    j = pl.program_id(1)
    tm = x_ref.shape[0]

    @pl.when(j == 0)
    def _():
        def body(r, carry):
            rows = pl.ds(pl.multiple_of(r * LN_ROWS, LN_ROWS), LN_ROWS)
            h_scr[rows, :] = _layer_norm(x_ref[rows, :], g_ref[...], b_ref[...]).astype(BF16)
            return carry
        lax.fori_loop(0, tm // LN_ROWS, body, 0)

    acc = jnp.dot(h_scr[...], w_ref[...].astype(BF16), preferred_element_type=F32)

    @pl.when(j == 0)
    def _():
        pool_ref[...] = acc

    @pl.when(j == 1)
    def _():
        qkv_ref[...] = (acc * q_scale).astype(BF16)

    @pl.when(j > 1)
    def _():
        qkv_ref[...] = acc.astype(BF16)


def _in_proj(x2, g, b, w_in, pool_width, q_scale):
    m, d = x2.shape
    n = w_in.shape[1]
    tm, tn = INPROJ_TM, INPROJ_TN
    assert pool_width == tn and n == 4 * tn and m % tm == 0
    return pl.pallas_call(
        functools.partial(_inproj_kernel, q_scale=q_scale),
        grid=(m // tm, n // tn),
        in_specs=[
            pl.BlockSpec((tm, d), lambda i, j: (i, 0)),
            pl.BlockSpec((1, d), lambda i, j: (0, 0)),
            pl.BlockSpec((1, d), lambda i, j: (0, 0)),
            pl.BlockSpec((d, tn), lambda i, j: (0, j)),
        ],
        out_specs=[
            pl.BlockSpec((tm, tn), lambda i, j: (i, 0)),
            pl.BlockSpec((tm, tn), lambda i, j: (i, jnp.maximum(j - 1, 0))),
        ],
        out_shape=[
            jax.ShapeDtypeStruct((m, pool_width), F32),
            jax.ShapeDtypeStruct((m, n - pool_width), BF16),
        ],
        scratch_shapes=[pltpu.VMEM((tm, d), BF16)],
        compiler_params=pltpu.CompilerParams(
            dimension_semantics=("arbitrary", "arbitrary"),
            vmem_limit_bytes=VMEM_LIMIT_BYTES),
        name="in_proj",
    )(x2, g, b, w_in)


def _attn_kernel(*refs, n_cast):
    q_ref, k_ref, v_ref, cum_ref = refs[:4]
    cast_in = refs[4:4 + n_cast]
    o_ref = refs[4 + n_cast]
    cast_out = refs[5 + n_cast:5 + 2 * n_cast]
    carry_scr, acc_scr, w_scr, a_scr = refs[5 + 2 * n_cast:]
    for src, dst in zip(cast_in, cast_out):
        dst[...] = src[...].astype(dst.dtype)
    qi = pl.program_id(2)
    t = ATTN_T
    nh = q_ref.shape[1] // HEAD_DIM
    skew = ATTN_SKEW
    heads = range(nh)
    cols = [slice(h * HEAD_DIM, (h + 1) * HEAD_DIM) for h in heads]

    def key_rows(kt):
        return pl.ds(pl.multiple_of(kt * t, t), t)

    def scores(kt, h):
        return lax.dot_general(q_ref[:, cols[h]], k_ref[key_rows(kt), cols[h]],
                               (((1,), (1,)), ((), ())),
                               preferred_element_type=F32)

    def deferred_values(kt, j):
        h = nh - skew + j
        acc_scr[h] += jnp.dot(a_scr[j], v_ref[key_rows(kt), cols[h]],
                              preferred_element_type=F32)

    def tile(kt, first):
        krows = key_rows(kt)
        if first:
            mask = (lax.broadcasted_iota(jnp.int32, (t, t), 1)
                    < lax.broadcasted_iota(jnp.int32, (t, t), 0))

        def softplus_cumsum(w):
            sp = jnp.maximum(
                w, jnp.log(1.0 + jnp.exp2(jnp.minimum(w, EXP2_CLAMP))) * LOG2_E)
            if first:
                sp = jnp.where(mask, sp, 0.0)
            return jnp.dot(sp.astype(BF16), cum_ref[...], preferred_element_type=F32)

        def weights_values(h, w, cs):
            total = jnp.broadcast_to(cs[:, 0:1], (t, HEAD_DIM))
            if first:
                a = jnp.where(mask, jnp.exp2(w - cs), 0.0)
            else:
                carry = carry_scr[h]
                a = jnp.exp2(w - cs - jnp.concatenate([carry] * (t // HEAD_DIM), axis=1))
            carry_scr[h] = total if first else carry + total
            if h >= nh - skew:
                a_scr[h - (nh - skew)] = a.astype(BF16)
                if first:
                    acc_scr[h] = jnp.zeros((t, HEAD_DIM), F32)
                return
            av = jnp.dot(a.astype(BF16), v_ref[krows, cols[h]], preferred_element_type=F32)
            if first:
                acc_scr[h] = av
            else:
                acc_scr[h] += av

        kt_next = jnp.maximum(kt - 1, 0)
        w, cs, w_next = {}, {}, {}
        for j in range(skew):
            w[j] = scores(kt, j) if first else w_scr[j]
        for h in heads:
            if h + skew < nh:
                w[h + skew] = scores(kt, h + skew)
            else:
                w_next[h + skew - nh] = scores(kt_next, h + skew - nh)
            cs[h] = softplus_cumsum(w[h])
            if h >= ATTN_LAG:
                weights_values(h - ATTN_LAG, w.pop(h - ATTN_LAG), cs.pop(h - ATTN_LAG))
            if h < skew and not first:
                deferred_values(kt + 1, h)
        for h in range(nh - ATTN_LAG, nh):
            weights_values(h, w.pop(h), cs.pop(h))
        for j in range(skew):
            w_scr[j] = w_next[j]

    tile(qi, True)

    def k_body(n, _):
        tile(qi - 1 - n, False)
        return 0

    lax.fori_loop(0, qi, k_body, 0)
    for j in range(skew):
        deferred_values(0, j)
    for h in heads:
        o_ref[:, cols[h]] = acc_scr[h].astype(o_ref.dtype)


def _cumsum_matrix():
    j = np.arange(ATTN_T)[:, None]
    s = np.arange(ATTN_T)[None, :]
    return jnp.asarray(j >= s, dtype=BF16)


def _sb_attention(qkv, batch, seq, heads, weights_f32):
    m = qkv.shape[0]
    nh = ATTN_HEADS
    t = ATTN_T
    assert heads % nh == 0 and seq % t == 0
    groups = heads // nh
    nq = seq // t
    width = nh * HEAD_DIM
    cum = _cumsum_matrix()
    steps = batch * groups * nq
    step = lambda b, g, i: ((b * groups + g) * nq + i, 0)
    slabs = []
    for wgt in weights_f32:
        rows, ncol = wgt.shape
        assert rows % (steps * 16) == 0
        slabs.append(pl.BlockSpec((rows // steps, ncol), step))
    out = pl.pallas_call(
        functools.partial(_attn_kernel, n_cast=len(weights_f32)),
        grid=(batch, groups, nq),
        in_specs=[
            pl.BlockSpec((t, width), lambda b, g, i: (b * nq + i, g)),
            pl.BlockSpec((seq, width), lambda b, g, i: (b, groups + g)),
            pl.BlockSpec((seq, width), lambda b, g, i: (b, 2 * groups + g)),
            pl.BlockSpec(cum.shape, lambda b, g, i: (0, 0)),
        ] + slabs,
        out_specs=[pl.BlockSpec((t, width), lambda b, g, i: (b * nq + i, g))] + slabs,
        out_shape=[jax.ShapeDtypeStruct((m, heads * HEAD_DIM), BF16)]
        + [jax.ShapeDtypeStruct(wgt.shape, BF16) for wgt in weights_f32],
        scratch_shapes=[pltpu.VMEM((nh, t, HEAD_DIM), F32), pltpu.VMEM((nh, t, HEAD_DIM), F32),
                        pltpu.VMEM((ATTN_SKEW, t, t), F32), pltpu.VMEM((ATTN_SKEW, t, t), BF16)],
        compiler_params=pltpu.CompilerParams(
            dimension_semantics=("arbitrary", "arbitrary", "arbitrary"),
            vmem_limit_bytes=VMEM_LIMIT_BYTES),
        name="sb_attn",
    )(qkv, qkv, qkv, cum, *weights_f32)
    return out[0], out[1:]


def _mix_kernel(x_ref, up_ref, halo_ref, ysb_ref, wout_ref, wpool_ref, pscale_ref,
                ling_ref, linb_ref, l1g_ref, l1b_ref, h1_ref, h1b_ref, *, seq, alpha):
    i = pl.program_id(0)
    tm = x_ref.shape[0]
    gw = wpool_ref.shape[1]
    blk_in_seq = i % (seq // tm)
    halo = jnp.where(blk_in_seq == 0, 0.0, halo_ref[...])
    u = up_ref[...]
    ext = jnp.concatenate([halo, u], axis=0)
    t = blk_in_seq * tm + lax.broadcasted_iota(jnp.int32, (tm, 1), 0)
    parts = []
    for g, w in enumerate(POOL_WINDOWS):
        cols = slice(g * gw, (g + 1) * gw)
        s = ext[:, cols]
        d = 1
        while d < w:
            s = s + pltpu.roll(s, d, axis=0)
            d *= 2
        s = s[POOL_HALO:, :]
        count = jnp.minimum(t + 1, w).astype(F32)
        y = s / count - u[:, cols]
        yp = jnp.dot(y.astype(BF16), wpool_ref[g], preferred_element_type=F32)
        parts.append((yp * pscale_ref[:, cols]).astype(BF16))
    mix_in = jnp.concatenate(parts + [ysb_ref[...]], axis=1)
    mix = jnp.dot(mix_in, wout_ref[...], preferred_element_type=F32)
    h = _layer_norm(x_ref[...], ling_ref[...], linb_ref[...])
    h1 = _layer_norm(alpha * h + mix, l1g_ref[...], l1b_ref[...])
    h1_ref[...] = h1
    h1b_ref[...] = h1.astype(BF16)


def _mix_ln1(x2, u_pool, y_sb, w_out_bf16, w_pool_bf16, pool_scale, ln_in_g, ln_in_b,
             ln1_g, ln1_b, seq, alpha):
    m, d = x2.shape
    pw = u_pool.shape[1]
    sw = y_sb.shape[1]
    tm = MIX_TM
    assert seq % tm == 0 and tm % POOL_HALO == 0 and max(POOL_WINDOWS) <= POOL_HALO
    halo_blocks = tm // POOL_HALO
    const = lambda i: (0, 0)
    return pl.pallas_call(
        functools.partial(_mix_kernel, seq=seq, alpha=alpha),
        grid=(m // tm,),
        in_specs=[
            pl.BlockSpec((tm, d), lambda i: (i, 0)),
            pl.BlockSpec((tm, pw), lambda i: (i, 0)),
            pl.BlockSpec((POOL_HALO, pw), lambda i: (jnp.maximum(i * halo_blocks - 1, 0), 0)),
            pl.BlockSpec((tm, sw), lambda i: (i, 0)),
            pl.BlockSpec(w_out_bf16.shape, const, pipeline_mode=pl.Buffered(1)),
            pl.BlockSpec(w_pool_bf16.shape, lambda i: (0, 0, 0)),
            pl.BlockSpec((1, pw), const),
            pl.BlockSpec((1, d), const),
            pl.BlockSpec((1, d), const),
            pl.BlockSpec((1, d), const),
            pl.BlockSpec((1, d), const),
        ],
        out_specs=[pl.BlockSpec((tm, d), lambda i: (i, 0))] * 2,
        out_shape=[jax.ShapeDtypeStruct((m, d), F32), jax.ShapeDtypeStruct((m, d), BF16)],
        compiler_params=pltpu.CompilerParams(
            dimension_semantics=("arbitrary",),
            vmem_limit_bytes=VMEM_LIMIT_BYTES),
        name="mix_ln1",
    )(x2, u_pool, u_pool, y_sb, w_out_bf16, w_pool_bf16, pool_scale,
      ln_in_g, ln_in_b, ln1_g, ln1_b)


def _ffn_kernel(h1_ref, h1b_ref, w1_ref, b1_ref, w2_ref, b2_ref, g_ref, b_ref, o_ref,
                acc_scr, *, alpha):
    i = pl.program_id(0)
    f = pl.program_id(1)
    last_i = pl.num_programs(0) - 1
    last_f = pl.num_programs(1) - 1

    def mlp_chunk():
        t = jnp.dot(h1b_ref[...], w1_ref[...], preferred_element_type=F32) + b1_ref[...]
        t = jnp.maximum(t, 0.0)
        return jnp.dot((t * t).astype(BF16), w2_ref[...], preferred_element_type=F32)

    def norm():
        return _layer_norm(acc_scr[...], g_ref[...], b_ref[...])

    @pl.when((f == 0) & (i == 0))
    def _():
        acc_scr[...] = mlp_chunk()

    @pl.when((f == 0) & (i > 0))
    def _():
        o_ref[...] = norm()
        acc_scr[...] = mlp_chunk()

    @pl.when((f > 0) & (f < last_f))
    def _():
        acc_scr[...] += mlp_chunk()

    @pl.when(f == last_f)
    def _():
        acc_scr[...] = alpha * h1_ref[...] + (acc_scr[...] + mlp_chunk() + b2_ref[...])

    @pl.when((f == last_f) & (i == last_i))
    def _():
        o_ref[...] = norm()


def _ffn_ln2(h1, h1_bf16, w1_bf16, b1, w2_bf16, b2, g, b, alpha):
    m, d = h1.shape
    dff = w1_bf16.shape[1]
    tm, tf = FFN_TM, FFN_TF
    assert m % tm == 0 and dff % tf == 0 and dff // tf >= 2
    n_i, n_f = m // tm, dff // tf
    const = lambda i, f: (0, 0)

    def out_block(i, f):
        final = (i == n_i - 1) & (f == n_f - 1)
        return (jnp.where(final, i, jnp.maximum(i - 1, 0)), 0)

    return pl.pallas_call(
        functools.partial(_ffn_kernel, alpha=alpha),
        grid=(n_i, n_f),
        in_specs=[
            pl.BlockSpec((tm, d), lambda i, f: (i, 0)),
            pl.BlockSpec((tm, d), lambda i, f: (i, 0)),
            pl.BlockSpec((d, tf), lambda i, f: (0, f)),
            pl.BlockSpec((1, tf), lambda i, f: (0, f)),
            pl.BlockSpec((tf, d), lambda i, f: (f, 0)),
            pl.BlockSpec((1, d), const),
            pl.BlockSpec((1, d), const),
            pl.BlockSpec((1, d), const),
        ],
        out_specs=pl.BlockSpec((tm, d), out_block),
        out_shape=jax.ShapeDtypeStruct((m, d), F32),
        scratch_shapes=[pltpu.VMEM((tm, d), F32)],
        compiler_params=pltpu.CompilerParams(
            dimension_semantics=("arbitrary", "arbitrary"),
            vmem_limit_bytes=VMEM_LIMIT_BYTES),
        name="ffn_ln2",
    )(h1, h1_bf16, w1_bf16, b1, w2_bf16, b2, g, b)


def kernel(x, ln_in_g, ln_in_b, w_in, w_pool, pool_scale, w_out, ln1_g, ln1_b,
           w_ff1, b_ff1, w_ff2, b_ff2, ln2_g, ln2_b):
    batch, seq, d = x.shape
    depth, groups, gw, _ = w_pool.shape
    assert depth == 1, "single trunk layer"
    pool_width = groups * gw
    sb_width = (w_in.shape[2] - pool_width) // 3
    heads = sb_width // HEAD_DIM
    alpha = float((2.0 * depth) ** 0.25)
    m = batch * seq

    x2 = x.reshape(m, d)
    row = lambda p: p.reshape(1, -1)

    q_scale = float(LOG2_E / np.sqrt(np.float32(HEAD_DIM)))
    u_pool, qkv = _in_proj(x2, row(ln_in_g), row(ln_in_b), w_in[0], pool_width, q_scale)
    y_sb, (w_out_bf16, w_ff1_bf16, w_ff2_bf16) = _sb_attention(
        qkv, batch, seq, heads, [w_out[0], w_ff1[0], w_ff2[0]])
    h1, h1_bf16 = _mix_ln1(x2, u_pool, y_sb, w_out_bf16, w_pool[0].astype(BF16),
                  row(pool_scale[0]), row(ln_in_g), row(ln_in_b), row(ln1_g[0]), row(ln1_b[0]),
                  seq, alpha)
    out = _ffn_ln2(h1, h1_bf16, w_ff1_bf16, row(b_ff1[0]), w_ff2_bf16,
                   row(b_ff2[0]), row(ln2_g[0]), row(ln2_b[0]), alpha)
    return out.reshape(batch, seq, d)
```

```python
import functools

import jax
import jax.numpy as jnp
import numpy as np
from jax import lax
from jax.experimental import pallas as pl
from jax.experimental.pallas import tpu as pltpu

F32 = jnp.float32
BF16 = jnp.bfloat16

LN_EPS = 1e-5
LOG2_E = 1.4426950408889634
EXP2_CLAMP = 100.0
POOL_WINDOWS = (2, 4, 8, 16)
POOL_HALO = 16
HEAD_DIM = 128

VMEM_LIMIT_BYTES = 56 * 1024 * 1024

INPROJ_TM = 1024
INPROJ_TN = 1024
ATTN_T = 256
ATTN_HEADS = 8
ATTN_SKEW = 2
ATTN_LAG = 4
MIX_TM = 512
FFN_TM = 512
FFN_TF = 1024
LN_ROWS = 128


def _layer_norm(x, g, b):
    mu = jnp.mean(x, axis=-1, keepdims=True)
    xc = x - mu
    var = jnp.mean(xc * xc, axis=-1, keepdims=True)
    return xc * lax.rsqrt(var + LN_EPS) * g + b


def _inproj_kernel(x_ref, g_ref, b_ref, w_ref, pool_ref, qkv_ref, h_scr, *, q_scale):
    j = pl.program_id(1)
    tm = x_ref.shape[0]

    @pl.when(j == 0)
    def _():
        def body(r, carry):
            rows = pl.ds(pl.multiple_of(r * LN_ROWS, LN_ROWS), LN_ROWS)
            h_scr[rows, :] = _layer_norm(x_ref[rows, :], g_ref[...], b_ref[...]).astype(BF16)
            return carry
        lax.fori_loop(0, tm // LN_ROWS, body, 0)

    acc = jnp.dot(h_scr[...], w_ref[...].astype(BF16), preferred_element_type=F32)

    @pl.when(j == 0)
    def _():
        pool_ref[...] = acc

    @pl.when(j == 1)
    def _():
        qkv_ref[...] = (acc * q_scale).astype(BF16)

    @pl.when(j > 1)
    def _():
        qkv_ref[...] = acc.astype(BF16)


def _in_proj(x2, g, b, w_in, pool_width, q_scale):
    m, d = x2.shape
    n = w_in.shape[1]
    tm, tn = INPROJ_TM, INPROJ_TN
    assert pool_width == tn and n == 4 * tn and m % tm == 0
    return pl.pallas_call(
        functools.partial(_inproj_kernel, q_scale=q_scale),
        grid=(m // tm, n // tn),
        in_specs=[
            pl.BlockSpec((tm, d), lambda i, j: (i, 0)),
            pl.BlockSpec((1, d), lambda i, j: (0, 0)),
            pl.BlockSpec((1, d), lambda i, j: (0, 0)),
            pl.BlockSpec((d, tn), lambda i, j: (0, j)),
        ],
        out_specs=[
            pl.BlockSpec((tm, tn), lambda i, j: (i, 0)),
            pl.BlockSpec((tm, tn), lambda i, j: (i, jnp.maximum(j - 1, 0))),
        ],
        out_shape=[
            jax.ShapeDtypeStruct((m, pool_width), F32),
            jax.ShapeDtypeStruct((m, n - pool_width), BF16),
        ],
        scratch_shapes=[pltpu.VMEM((tm, d), BF16)],
        compiler_params=pltpu.CompilerParams(
            dimension_semantics=("arbitrary", "arbitrary"),
            vmem_limit_bytes=VMEM_LIMIT_BYTES),
        name="in_proj",
    )(x2, g, b, w_in)


def _attn_kernel(*refs, n_cast):
    q_ref, k_ref, v_ref, cum_ref = refs[:4]
    cast_in = refs[4:4 + n_cast]
    o_ref = refs[4 + n_cast]
    cast_out = refs[5 + n_cast:5 + 2 * n_cast]
    carry_scr, acc_scr, w_scr, a_scr = refs[5 + 2 * n_cast:]
    for src, dst in zip(cast_in, cast_out):
        dst[...] = src[...].astype(dst.dtype)
    qi = pl.program_id(2)
    t = ATTN_T
    nh = q_ref.shape[1] // HEAD_DIM
    skew = ATTN_SKEW
    heads = range(nh)
    cols = [slice(h * HEAD_DIM, (h + 1) * HEAD_DIM) for h in heads]

    def key_rows(kt):
        return pl.ds(pl.multiple_of(kt * t, t), t)

    def scores(kt, h):
        return lax.dot_general(q_ref[:, cols[h]], k_ref[key_rows(kt), cols[h]],
                               (((1,), (1,)), ((), ())),
                               preferred_element_type=F32)

    def deferred_values(kt, j):
        h = nh - skew + j
        acc_scr[h] += jnp.dot(a_scr[j], v_ref[key_rows(kt), cols[h]],
                              preferred_element_type=F32)

    def sweep(kts, first):
        n_items = len(kts) * nh
        item = lambda n: (kts[n // nh], n % nh)
        kt_after = jnp.maximum(kts[-1] - 1, 0)
        if first:
            mask = (lax.broadcasted_iota(jnp.int32, (t, t), 1)
                    < lax.broadcasted_iota(jnp.int32, (t, t), 0))

        def softplus_cumsum(w):
            sp = jnp.maximum(
                w, jnp.log(1.0 + jnp.exp2(jnp.minimum(w, EXP2_CLAMP))) * LOG2_E)
            if first:
                sp = jnp.where(mask, sp, 0.0)
            return jnp.dot(sp.astype(BF16), cum_ref[...], preferred_element_type=F32)

        def weights_values(n, w, cs):
            kt, h = item(n)
            total = jnp.broadcast_to(cs[:, 0:1], (t, HEAD_DIM))
            if first:
                a = jnp.where(mask, jnp.exp2(w - cs), 0.0)
            else:
                carry = carry_scr[h]
                a = jnp.exp2(w - cs - jnp.concatenate([carry] * (t // HEAD_DIM), axis=1))
            carry_scr[h] = total if first else carry + total
            if n >= n_items - skew:
                a_scr[n - (n_items - skew)] = a.astype(BF16)
                if first:
                    acc_scr[h] = jnp.zeros((t, HEAD_DIM), F32)
                return
            av = jnp.dot(a.astype(BF16), v_ref[key_rows(kt), cols[h]],
                         preferred_element_type=F32)
            if first:
                acc_scr[h] = av
            else:
                acc_scr[h] += av

        w, cs, w_after = {}, {}, {}
        for j in range(skew):
            w[j] = scores(*item(j)) if first else w_scr[j]
        for n in range(n_items):
            if n + skew < n_items:
                w[n + skew] = scores(*item(n + skew))
            else:
                w_after[n + skew - n_items] = scores(kt_after, n + skew - n_items)
            cs[n] = softplus_cumsum(w[n])
            if n >= ATTN_LAG:
                weights_values(n - ATTN_LAG, w.pop(n - ATTN_LAG), cs.pop(n - ATTN_LAG))
            if n < skew and not first:
                deferred_values(kts[0] + 1, n)
        for n in range(n_items - ATTN_LAG, n_items):
            weights_values(n, w.pop(n), cs.pop(n))
        for j in range(skew):
            w_scr[j] = w_after[j]

    sweep([qi], True)

    def pair_body(n, _):
        kt = qi - 1 - 2 * n
        sweep([kt, kt - 1], False)
        return 0

    lax.fori_loop(0, qi // 2, pair_body, 0)

    @pl.when(qi % 2 == 1)
    def _():
        sweep([qi * 0], False)

    for j in range(skew):
        deferred_values(0, j)
    for h in heads:
        o_ref[:, cols[h]] = acc_scr[h].astype(o_ref.dtype)


def _cumsum_matrix():
    j = np.arange(ATTN_T)[:, None]
    s = np.arange(ATTN_T)[None, :]
    return jnp.asarray(j >= s, dtype=BF16)


def _sb_attention(qkv, batch, seq, heads, weights_f32):
    m = qkv.shape[0]
    nh = ATTN_HEADS
    t = ATTN_T
    assert heads % nh == 0 and seq % t == 0
    groups = heads // nh
    nq = seq // t
    width = nh * HEAD_DIM
    cum = _cumsum_matrix()
    steps = batch * groups * nq
    step = lambda b, g, i: ((b * groups + g) * nq + i, 0)
    slabs = []
    for wgt in weights_f32:
        rows, ncol = wgt.shape
        assert rows % (steps * 16) == 0
        slabs.append(pl.BlockSpec((rows // steps, ncol), step))
    out = pl.pallas_call(
        functools.partial(_attn_kernel, n_cast=len(weights_f32)),
        grid=(batch, groups, nq),
        in_specs=[
            pl.BlockSpec((t, width), lambda b, g, i: (b * nq + i, g)),
            pl.BlockSpec((seq, width), lambda b, g, i: (b, groups + g)),
            pl.BlockSpec((seq, width), lambda b, g, i: (b, 2 * groups + g)),
            pl.BlockSpec(cum.shape, lambda b, g, i: (0, 0)),
        ] + slabs,
        out_specs=[pl.BlockSpec((t, width), lambda b, g, i: (b * nq + i, g))] + slabs,
        out_shape=[jax.ShapeDtypeStruct((m, heads * HEAD_DIM), BF16)]
        + [jax.ShapeDtypeStruct(wgt.shape, BF16) for wgt in weights_f32],
        scratch_shapes=[pltpu.VMEM((nh, t, HEAD_DIM), F32), pltpu.VMEM((nh, t, HEAD_DIM), F32),
                        pltpu.VMEM((ATTN_SKEW, t, t), F32), pltpu.VMEM((ATTN_SKEW, t, t), BF16)],
        compiler_params=pltpu.CompilerParams(
            dimension_semantics=("arbitrary", "arbitrary", "arbitrary"),
            vmem_limit_bytes=VMEM_LIMIT_BYTES),
        name="sb_attn",
    )(qkv, qkv, qkv, cum, *weights_f32)
    return out[0], out[1:]


def _mix_kernel(x_ref, up_ref, halo_ref, ysb_ref, wout_ref, wpool_ref, pscale_ref,
                ling_ref, linb_ref, l1g_ref, l1b_ref, h1_ref, h1b_ref, *, seq, alpha):
    i = pl.program_id(0)
    tm = x_ref.shape[0]
    gw = wpool_ref.shape[1]
    blk_in_seq = i % (seq // tm)
    halo = jnp.where(blk_in_seq == 0, 0.0, halo_ref[...])
    u = up_ref[...]
    ext = jnp.concatenate([halo, u], axis=0)
    t = blk_in_seq * tm + lax.broadcasted_iota(jnp.int32, (tm, 1), 0)
    parts = []
    for g, w in enumerate(POOL_WINDOWS):
        cols = slice(g * gw, (g + 1) * gw)
        s = ext[:, cols]
        d = 1
        while d < w:
            s = s + pltpu.roll(s, d, axis=0)
            d *= 2
        s = s[POOL_HALO:, :]
        count = jnp.minimum(t + 1, w).astype(F32)
        y = s / count - u[:, cols]
        yp = jnp.dot(y.astype(BF16), wpool_ref[g], preferred_element_type=F32)
        parts.append((yp * pscale_ref[:, cols]).astype(BF16))
    mix_in = jnp.concatenate(parts + [ysb_ref[...]], axis=1)
    mix = jnp.dot(mix_in, wout_ref[...], preferred_element_type=F32)
    h = _layer_norm(x_ref[...], ling_ref[...], linb_ref[...])
    h1 = _layer_norm(alpha * h + mix, l1g_ref[...], l1b_ref[...])
    h1_ref[...] = h1
    h1b_ref[...] = h1.astype(BF16)


def _mix_ln1(x2, u_pool, y_sb, w_out_bf16, w_pool_bf16, pool_scale, ln_in_g, ln_in_b,
             ln1_g, ln1_b, seq, alpha):
    m, d = x2.shape
    pw = u_pool.shape[1]
    sw = y_sb.shape[1]
    tm = MIX_TM
    assert seq % tm == 0 and tm % POOL_HALO == 0 and max(POOL_WINDOWS) <= POOL_HALO
    halo_blocks = tm // POOL_HALO
    const = lambda i: (0, 0)
    return pl.pallas_call(
        functools.partial(_mix_kernel, seq=seq, alpha=alpha),
        grid=(m // tm,),
        in_specs=[
            pl.BlockSpec((tm, d), lambda i: (i, 0)),
            pl.BlockSpec((tm, pw), lambda i: (i, 0)),
            pl.BlockSpec((POOL_HALO, pw), lambda i: (jnp.maximum(i * halo_blocks - 1, 0), 0)),
            pl.BlockSpec((tm, sw), lambda i: (i, 0)),
            pl.BlockSpec(w_out_bf16.shape, const, pipeline_mode=pl.Buffered(1)),
            pl.BlockSpec(w_pool_bf16.shape, lambda i: (0, 0, 0)),
            pl.BlockSpec((1, pw), const),
            pl.BlockSpec((1, d), const),
            pl.BlockSpec((1, d), const),
            pl.BlockSpec((1, d), const),
            pl.BlockSpec((1, d), const),
        ],
        out_specs=[pl.BlockSpec((tm, d), lambda i: (i, 0))] * 2,
        out_shape=[jax.ShapeDtypeStruct((m, d), F32), jax.ShapeDtypeStruct((m, d), BF16)],
        compiler_params=pltpu.CompilerParams(
            dimension_semantics=("arbitrary",),
            vmem_limit_bytes=VMEM_LIMIT_BYTES),
        name="mix_ln1",
    )(x2, u_pool, u_pool, y_sb, w_out_bf16, w_pool_bf16, pool_scale,
      ln_in_g, ln_in_b, ln1_g, ln1_b)


def _ffn_kernel(h1_ref, h1b_ref, w1_ref, b1_ref, w2_ref, b2_ref, g_ref, b_ref, o_ref,
                acc_scr, *, alpha):
    i = pl.program_id(0)
    f = pl.program_id(1)
    last_i = pl.num_programs(0) - 1
    last_f = pl.num_programs(1) - 1

    def mlp_chunk():
        t = jnp.dot(h1b_ref[...], w1_ref[...], preferred_element_type=F32) + b1_ref[...]
        t = jnp.maximum(t, 0.0)
        return jnp.dot((t * t).astype(BF16), w2_ref[...], preferred_element_type=F32)

    def norm():
        return _layer_norm(acc_scr[...], g_ref[...], b_ref[...])

    @pl.when((f == 0) & (i == 0))
    def _():
        acc_scr[...] = mlp_chunk()

    @pl.when((f == 0) & (i > 0))
    def _():
        o_ref[...] = norm()
        acc_scr[...] = mlp_chunk()

    @pl.when((f > 0) & (f < last_f))
    def _():
        acc_scr[...] += mlp_chunk()

    @pl.when(f == last_f)
    def _():
        acc_scr[...] = alpha * h1_ref[...] + (acc_scr[...] + mlp_chunk() + b2_ref[...])

    @pl.when((f == last_f) & (i == last_i))
    def _():
        o_ref[...] = norm()


def _ffn_ln2(h1, h1_bf16, w1_bf16, b1, w2_bf16, b2, g, b, alpha):
    m, d = h1.shape
    dff = w1_bf16.shape[1]
    tm, tf = FFN_TM, FFN_TF
    assert m % tm == 0 and dff % tf == 0 and dff // tf >= 2
    n_i, n_f = m // tm, dff // tf
    const = lambda i, f: (0, 0)

    def out_block(i, f):
        final = (i == n_i - 1) & (f == n_f - 1)
        return (jnp.where(final, i, jnp.maximum(i - 1, 0)), 0)

    return pl.pallas_call(
        functools.partial(_ffn_kernel, alpha=alpha),
        grid=(n_i, n_f),
        in_specs=[
            pl.BlockSpec((tm, d), lambda i, f: (i, 0)),
            pl.BlockSpec((tm, d), lambda i, f: (i, 0)),
            pl.BlockSpec((d, tf), lambda i, f: (0, f)),
            pl.BlockSpec((1, tf), lambda i, f: (0, f)),
            pl.BlockSpec((tf, d), lambda i, f: (f, 0)),
            pl.BlockSpec((1, d), const),
            pl.BlockSpec((1, d), const),
            pl.BlockSpec((1, d), const),
        ],
        out_specs=pl.BlockSpec((tm, d), out_block),
        out_shape=jax.ShapeDtypeStruct((m, d), F32),
        scratch_shapes=[pltpu.VMEM((tm, d), F32)],
        compiler_params=pltpu.CompilerParams(
            dimension_semantics=("arbitrary", "arbitrary"),
            vmem_limit_bytes=VMEM_LIMIT_BYTES),
        name="ffn_ln2",
    )(h1, h1_bf16, w1_bf16, b1, w2_bf16, b2, g, b)


def kernel(x, ln_in_g, ln_in_b, w_in, w_pool, pool_scale, w_out, ln1_g, ln1_b,
           w_ff1, b_ff1, w_ff2, b_ff2, ln2_g, ln2_b):
    batch, seq, d = x.shape
    depth, groups, gw, _ = w_pool.shape
    assert depth == 1, "single trunk layer"
    pool_width = groups * gw
    sb_width = (w_in.shape[2] - pool_width) // 3
    heads = sb_width // HEAD_DIM
    alpha = float((2.0 * depth) ** 0.25)
    m = batch * seq

    x2 = x.reshape(m, d)
    row = lambda p: p.reshape(1, -1)

    q_scale = float(LOG2_E / np.sqrt(np.float32(HEAD_DIM)))
    u_pool, qkv = _in_proj(x2, row(ln_in_g), row(ln_in_b), w_in[0], pool_width, q_scale)
    y_sb, (w_out_bf16, w_ff1_bf16, w_ff2_bf16) = _sb_attention(
        qkv, batch, seq, heads, [w_out[0], w_ff1[0], w_ff2[0]])
    h1, h1_bf16 = _mix_ln1(x2, u_pool, y_sb, w_out_bf16, w_pool[0].astype(BF16),
                  row(pool_scale[0]), row(ln_in_g), row(ln_in_b), row(ln1_g[0]), row(ln1_b[0]),
                  seq, alpha)
    out = _ffn_ln2(h1, h1_bf16, w_ff1_bf16, row(b_ff1[0]), w_ff2_bf16,
                   row(b_ff2[0]), row(ln2_g[0]), row(ln2_b[0]), alpha)
    return out.reshape(batch, seq, d)
```

```python
import functools

import jax
import jax.numpy as jnp
import numpy as np
from jax import lax
from jax.experimental import pallas as pl
from jax.experimental.pallas import tpu as pltpu

F32 = jnp.float32
BF16 = jnp.bfloat16

LN_EPS = 1e-5
LOG2_E = 1.4426950408889634
EXP2_CLAMP = 100.0
POOL_WINDOWS = (2, 4, 8, 16)
POOL_HALO = 16
HEAD_DIM = 128

VMEM_LIMIT_BYTES = 56 * 1024 * 1024
FFN_VMEM_LIMIT_BYTES = 60 * 1024 * 1024

INPROJ_TM = 1024
INPROJ_TN = 1024
ATTN_T = 256
ATTN_HEADS = 8
ATTN_SKEW = 2
ATTN_LAG = 4
MIX_TM = 512
FFN_TM = 512
FFN_TF = 2048
LN_ROWS = 128


def _layer_norm(x, g, b):
    mu = jnp.mean(x, axis=-1, keepdims=True)
    xc = x - mu
    var = jnp.mean(xc * xc, axis=-1, keepdims=True)
    return xc * lax.rsqrt(var + LN_EPS) * g + b


def _inproj_kernel(x_ref, g_ref, b_ref, w_ref, pool_ref, qkv_ref, h_scr, *, q_scale):
    j = pl.program_id(1)
    tm = x_ref.shape[0]

    @pl.when(j == 0)
    def _():
        def body(r, carry):
            rows = pl.ds(pl.multiple_of(r * LN_ROWS, LN_ROWS), LN_ROWS)
            h_scr[rows, :] = _layer_norm(x_ref[rows, :], g_ref[...], b_ref[...]).astype(BF16)
            return carry
        lax.fori_loop(0, tm // LN_ROWS, body, 0)

    acc = jnp.dot(h_scr[...], w_ref[...].astype(BF16), preferred_element_type=F32)

    @pl.when(j == 0)
    def _():
        pool_ref[...] = acc

    @pl.when(j == 1)
    def _():
        qkv_ref[...] = (acc * q_scale).astype(BF16)

    @pl.when(j > 1)
    def _():
        qkv_ref[...] = acc.astype(BF16)


def _in_proj(x2, g, b, w_in, pool_width, q_scale):
    m, d = x2.shape
    n = w_in.shape[1]
    tm, tn = INPROJ_TM, INPROJ_TN
    assert pool_width == tn and n == 4 * tn and m % tm == 0
    return pl.pallas_call(
        functools.partial(_inproj_kernel, q_scale=q_scale),
        grid=(m // tm, n // tn),
        in_specs=[
            pl.BlockSpec((tm, d), lambda i, j: (i, 0)),
            pl.BlockSpec((1, d), lambda i, j: (0, 0)),
            pl.BlockSpec((1, d), lambda i, j: (0, 0)),
            pl.BlockSpec((d, tn), lambda i, j: (0, j)),
        ],
        out_specs=[
            pl.BlockSpec((tm, tn), lambda i, j: (i, 0)),
            pl.BlockSpec((tm, tn), lambda i, j: (i, jnp.maximum(j - 1, 0))),
        ],
        out_shape=[
            jax.ShapeDtypeStruct((m, pool_width), F32),
            jax.ShapeDtypeStruct((m, n - pool_width), BF16),
        ],
        scratch_shapes=[pltpu.VMEM((tm, d), BF16)],
        compiler_params=pltpu.CompilerParams(
            dimension_semantics=("arbitrary", "arbitrary"),
            vmem_limit_bytes=VMEM_LIMIT_BYTES),
        name="in_proj",
    )(x2, g, b, w_in)


def _attn_kernel(*refs, n_cast):
    q_ref, k_ref, v_ref, cum_ref = refs[:4]
    cast_in = refs[4:4 + n_cast]
    o_ref = refs[4 + n_cast]
    cast_out = refs[5 + n_cast:5 + 2 * n_cast]
    carry_scr, acc_scr, w_scr, a_scr = refs[5 + 2 * n_cast:]
    for src, dst in zip(cast_in, cast_out):
        dst[...] = src[...].astype(dst.dtype)
    qi = pl.program_id(2)
    t = ATTN_T
    nh = q_ref.shape[1] // HEAD_DIM
    skew = ATTN_SKEW
    heads = range(nh)
    cols = [slice(h * HEAD_DIM, (h + 1) * HEAD_DIM) for h in heads]

    def key_rows(kt):
        return pl.ds(pl.multiple_of(kt * t, t), t)

    def scores(kt, h):
        return lax.dot_general(q_ref[:, cols[h]], k_ref[key_rows(kt), cols[h]],
                               (((1,), (1,)), ((), ())),
                               preferred_element_type=F32)

    def deferred_values(kt, j):
        h = nh - skew + j
        acc_scr[h] += jnp.dot(a_scr[j], v_ref[key_rows(kt), cols[h]],
                              preferred_element_type=F32)

    def sweep(kts, first):
        n_items = len(kts) * nh
        item = lambda n: (kts[n // nh], n % nh)
        kt_after = jnp.maximum(kts[-1] - 1, 0)
        if first:
            mask = (lax.broadcasted_iota(jnp.int32, (t, t), 1)
                    < lax.broadcasted_iota(jnp.int32, (t, t), 0))

        def softplus_cumsum(w):
            sp = jnp.maximum(
                w, jnp.log(1.0 + jnp.exp2(jnp.minimum(w, EXP2_CLAMP))) * LOG2_E)
            if first:
                sp = jnp.where(mask, sp, 0.0)
            return jnp.dot(sp.astype(BF16), cum_ref[...], preferred_element_type=F32)

        def weights_values(n, w, cs):
            kt, h = item(n)
            total = jnp.broadcast_to(cs[:, 0:1], (t, HEAD_DIM))
            if first:
                a = jnp.where(mask, jnp.exp2(w - cs), 0.0)
            else:
                carry = carry_scr[h]
                a = jnp.exp2(w - cs - jnp.concatenate([carry] * (t // HEAD_DIM), axis=1))
            carry_scr[h] = total if first else carry + total
            if n >= n_items - skew:
                a_scr[n - (n_items - skew)] = a.astype(BF16)
                if first:
                    acc_scr[h] = jnp.zeros((t, HEAD_DIM), F32)
                return
            av = jnp.dot(a.astype(BF16), v_ref[key_rows(kt), cols[h]],
                         preferred_element_type=F32)
            if first:
                acc_scr[h] = av
            else:
                acc_scr[h] += av

        w, cs, w_after = {}, {}, {}
        for j in range(skew):
            w[j] = scores(*item(j)) if first else w_scr[j]
        for n in range(n_items):
            if n + skew < n_items:
                w[n + skew] = scores(*item(n + skew))
            else:
                w_after[n + skew - n_items] = scores(kt_after, n + skew - n_items)
            cs[n] = softplus_cumsum(w[n])
            if n >= ATTN_LAG:
                weights_values(n - ATTN_LAG, w.pop(n - ATTN_LAG), cs.pop(n - ATTN_LAG))
            if n < skew and not first:
                deferred_values(kts[0] + 1, n)
        for n in range(n_items - ATTN_LAG, n_items):
            weights_values(n, w.pop(n), cs.pop(n))
        for j in range(skew):
            w_scr[j] = w_after[j]

    sweep([qi], True)

    def pair_body(n, _):
        kt = qi - 1 - 2 * n
        sweep([kt, kt - 1], False)
        return 0

    lax.fori_loop(0, qi // 2, pair_body, 0)

    @pl.when(qi % 2 == 1)
    def _():
        sweep([qi * 0], False)

    for j in range(skew):
        deferred_values(0, j)
    for h in heads:
        o_ref[:, cols[h]] = acc_scr[h].astype(o_ref.dtype)


def _cumsum_matrix():
    j = np.arange(ATTN_T)[:, None]
    s = np.arange(ATTN_T)[None, :]
    return jnp.asarray(j >= s, dtype=BF16)


def _sb_attention(qkv, batch, seq, heads, weights_f32):
    m = qkv.shape[0]
    nh = ATTN_HEADS
    t = ATTN_T
    assert heads % nh == 0 and seq % t == 0
    groups = heads // nh
    nq = seq // t
    width = nh * HEAD_DIM
    cum = _cumsum_matrix()
    steps = batch * groups * nq
    step = lambda b, g, i: ((b * groups + g) * nq + i, 0)
    slabs = []
    for wgt in weights_f32:
        rows, ncol = wgt.shape
        assert rows % (steps * 16) == 0
        slabs.append(pl.BlockSpec((rows // steps, ncol), step))
    out = pl.pallas_call(
        functools.partial(_attn_kernel, n_cast=len(weights_f32)),
        grid=(batch, groups, nq),
        in_specs=[
            pl.BlockSpec((t, width), lambda b, g, i: (b * nq + i, g)),
            pl.BlockSpec((seq, width), lambda b, g, i: (b, groups + g)),
            pl.BlockSpec((seq, width), lambda b, g, i: (b, 2 * groups + g)),
            pl.BlockSpec(cum.shape, lambda b, g, i: (0, 0)),
        ] + slabs,
        out_specs=[pl.BlockSpec((t, width), lambda b, g, i: (b * nq + i, g))] + slabs,
        out_shape=[jax.ShapeDtypeStruct((m, heads * HEAD_DIM), BF16)]
        + [jax.ShapeDtypeStruct(wgt.shape, BF16) for wgt in weights_f32],
        scratch_shapes=[pltpu.VMEM((nh, t, HEAD_DIM), F32), pltpu.VMEM((nh, t, HEAD_DIM), F32),
                        pltpu.VMEM((ATTN_SKEW, t, t), F32), pltpu.VMEM((ATTN_SKEW, t, t), BF16)],
        compiler_params=pltpu.CompilerParams(
            dimension_semantics=("arbitrary", "arbitrary", "arbitrary"),
            vmem_limit_bytes=VMEM_LIMIT_BYTES),
        name="sb_attn",
    )(qkv, qkv, qkv, cum, *weights_f32)
    return out[0], out[1:]


def _mix_kernel(x_ref, up_ref, halo_ref, ysb_ref, wout_ref, wpool_ref, pscale_ref,
                ling_ref, linb_ref, l1g_ref, l1b_ref, h1_ref, h1b_ref, *, seq, alpha):
    i = pl.program_id(0)
    tm = x_ref.shape[0]
    gw = wpool_ref.shape[1]
    blk_in_seq = i % (seq // tm)
    halo = jnp.where(blk_in_seq == 0, 0.0, halo_ref[...])
    u = up_ref[...]
    ext = jnp.concatenate([halo, u], axis=0)
    t = blk_in_seq * tm + lax.broadcasted_iota(jnp.int32, (tm, 1), 0)
    parts = []
    for g, w in enumerate(POOL_WINDOWS):
        cols = slice(g * gw, (g + 1) * gw)
        s = ext[:, cols]
        d = 1
        while d < w:
            s = s + pltpu.roll(s, d, axis=0)
            d *= 2
        s = s[POOL_HALO:, :]
        count = jnp.minimum(t + 1, w).astype(F32)
        y = s / count - u[:, cols]
        yp = jnp.dot(y.astype(BF16), wpool_ref[g], preferred_element_type=F32)
        parts.append((yp * pscale_ref[:, cols]).astype(BF16))
    mix_in = jnp.concatenate(parts + [ysb_ref[...]], axis=1)
    mix = jnp.dot(mix_in, wout_ref[...], preferred_element_type=F32)
    h = _layer_norm(x_ref[...], ling_ref[...], linb_ref[...])
    h1 = _layer_norm(alpha * h + mix, l1g_ref[...], l1b_ref[...])
    h1_ref[...] = h1
    h1b_ref[...] = h1.astype(BF16)


def _mix_ln1(x2, u_pool, y_sb, w_out_bf16, w_pool_bf16, pool_scale, ln_in_g, ln_in_b,
             ln1_g, ln1_b, seq, alpha):
    m, d = x2.shape
    pw = u_pool.shape[1]
    sw = y_sb.shape[1]
    tm = MIX_TM
    assert seq % tm == 0 and tm % POOL_HALO == 0 and max(POOL_WINDOWS) <= POOL_HALO
    halo_blocks = tm // POOL_HALO
    const = lambda i: (0, 0)
    return pl.pallas_call(
        functools.partial(_mix_kernel, seq=seq, alpha=alpha),
        grid=(m // tm,),
        in_specs=[
            pl.BlockSpec((tm, d), lambda i: (i, 0)),
            pl.BlockSpec((tm, pw), lambda i: (i, 0)),
            pl.BlockSpec((POOL_HALO, pw), lambda i: (jnp.maximum(i * halo_blocks - 1, 0), 0)),
            pl.BlockSpec((tm, sw), lambda i: (i, 0)),
            pl.BlockSpec(w_out_bf16.shape, const, pipeline_mode=pl.Buffered(1)),
            pl.BlockSpec(w_pool_bf16.shape, lambda i: (0, 0, 0)),
            pl.BlockSpec((1, pw), const),
            pl.BlockSpec((1, d), const),
            pl.BlockSpec((1, d), const),
            pl.BlockSpec((1, d), const),
            pl.BlockSpec((1, d), const),
        ],
        out_specs=[pl.BlockSpec((tm, d), lambda i: (i, 0))] * 2,
        out_shape=[jax.ShapeDtypeStruct((m, d), F32), jax.ShapeDtypeStruct((m, d), BF16)],
        compiler_params=pltpu.CompilerParams(
            dimension_semantics=("arbitrary",),
            vmem_limit_bytes=VMEM_LIMIT_BYTES),
        name="mix_ln1",
    )(x2, u_pool, u_pool, y_sb, w_out_bf16, w_pool_bf16, pool_scale,
      ln_in_g, ln_in_b, ln1_g, ln1_b)


def _ffn_kernel(h1_ref, h1b_ref, w1_ref, b1_ref, w2_ref, b2_ref, g_ref, b_ref, o_ref,
                acc_scr, *, alpha):
    i = pl.program_id(0)
    f = pl.program_id(1)
    last_i = pl.num_programs(0) - 1
    last_f = pl.num_programs(1) - 1

    def mlp_chunk():
        t = jnp.dot(h1b_ref[...], w1_ref[...], preferred_element_type=F32) + b1_ref[...]
        t = jnp.maximum(t, 0.0)
        return jnp.dot((t * t).astype(BF16), w2_ref[...], preferred_element_type=F32)

    def norm():
        return _layer_norm(acc_scr[...], g_ref[...], b_ref[...])

    @pl.when((f == 0) & (i == 0))
    def _():
        acc_scr[...] = mlp_chunk()

    @pl.when((f == 0) & (i > 0))
    def _():
        o_ref[...] = norm()
        acc_scr[...] = mlp_chunk()

    @pl.when((f > 0) & (f < last_f))
    def _():
        acc_scr[...] += mlp_chunk()

    @pl.when(f == last_f)
    def _():
        acc_scr[...] = alpha * h1_ref[...] + (acc_scr[...] + mlp_chunk() + b2_ref[...])

    @pl.when((f == last_f) & (i == last_i))
    def _():
        o_ref[...] = norm()


def _ffn_ln2(h1, h1_bf16, w1_bf16, b1, w2_bf16, b2, g, b, alpha):
    m, d = h1.shape
    dff = w1_bf16.shape[1]
    tm, tf = FFN_TM, FFN_TF
    assert m % tm == 0 and dff % tf == 0 and dff // tf >= 2
    n_i, n_f = m // tm, dff // tf
    const = lambda i, f: (0, 0)

    def out_block(i, f):
        final = (i == n_i - 1) & (f == n_f - 1)
        return (jnp.where(final, i, jnp.maximum(i - 1, 0)), 0)

    return pl.pallas_call(
        functools.partial(_ffn_kernel, alpha=alpha),
        grid=(n_i, n_f),
        in_specs=[
            pl.BlockSpec((tm, d), lambda i, f: (i, 0), pipeline_mode=pl.Buffered(1)),
            pl.BlockSpec((tm, d), lambda i, f: (i, 0)),
            pl.BlockSpec((d, tf), lambda i, f: (0, f)),
            pl.BlockSpec((1, tf), lambda i, f: (0, f)),
            pl.BlockSpec((tf, d), lambda i, f: (f, 0)),
            pl.BlockSpec((1, d), const),
            pl.BlockSpec((1, d), const),
            pl.BlockSpec((1, d), const),
        ],
        out_specs=pl.BlockSpec((tm, d), out_block),
        out_shape=jax.ShapeDtypeStruct((m, d), F32),
        scratch_shapes=[pltpu.VMEM((tm, d), F32)],
        compiler_params=pltpu.CompilerParams(
            dimension_semantics=("arbitrary", "arbitrary"),
            vmem_limit_bytes=FFN_VMEM_LIMIT_BYTES),
        name="ffn_ln2",
    )(h1, h1_bf16, w1_bf16, b1, w2_bf16, b2, g, b)


def kernel(x, ln_in_g, ln_in_b, w_in, w_pool, pool_scale, w_out, ln1_g, ln1_b,
           w_ff1, b_ff1, w_ff2, b_ff2, ln2_g, ln2_b):
    batch, seq, d = x.shape
    depth, groups, gw, _ = w_pool.shape
    assert depth == 1, "single trunk layer"
    pool_width = groups * gw
    sb_width = (w_in.shape[2] - pool_width) // 3
    heads = sb_width // HEAD_DIM
    alpha = float((2.0 * depth) ** 0.25)
    m = batch * seq

    x2 = x.reshape(m, d)
    row = lambda p: p.reshape(1, -1)

    q_scale = float(LOG2_E / np.sqrt(np.float32(HEAD_DIM)))
    u_pool, qkv = _in_proj(x2, row(ln_in_g), row(ln_in_b), w_in[0], pool_width, q_scale)
    y_sb, (w_out_bf16, w_ff1_bf16, w_ff2_bf16) = _sb_attention(
        qkv, batch, seq, heads, [w_out[0], w_ff1[0], w_ff2[0]])
    h1, h1_bf16 = _mix_ln1(x2, u_pool, y_sb, w_out_bf16, w_pool[0].astype(BF16),
                  row(pool_scale[0]), row(ln_in_g), row(ln_in_b), row(ln1_g[0]), row(ln1_b[0]),
                  seq, alpha)
    out = _ffn_ln2(h1, h1_bf16, w_ff1_bf16, row(b_ff1[0]), w_ff2_bf16,
                   row(b_ff2[0]), row(ln2_g[0]), row(ln2_b[0]), alpha)
    return out.reshape(batch, seq, d)
```

```python
import functools

import jax
import jax.numpy as jnp
import numpy as np
from jax import lax
from jax.experimental import pallas as pl
from jax.experimental.pallas import tpu as pltpu

F32 = jnp.float32
BF16 = jnp.bfloat16

LN_EPS = 1e-5
LOG2_E = 1.4426950408889634
EXP2_CLAMP = 100.0
POOL_WINDOWS = (2, 4, 8, 16)
POOL_HALO = 16
HEAD_DIM = 128

VMEM_LIMIT_BYTES = 56 * 1024 * 1024

INPROJ_TM = 1024
INPROJ_TN = 1024
ATTN_T = 256
ATTN_HEADS = 8
ATTN_SKEW = 2
ATTN_LAG = 4
MIX_TM = 512
FFN_TM = 512
FFN_TF = 512
LN_ROWS = 128


def _layer_norm(x, g, b):
    mu = jnp.mean(x, axis=-1, keepdims=True)
    xc = x - mu
    var = jnp.mean(xc * xc, axis=-1, keepdims=True)
    return xc * lax.rsqrt(var + LN_EPS) * g + b


def _inproj_kernel(x_ref, g_ref, b_ref, w_ref, pool_ref, qkv_ref, h_scr, *, q_scale):
    j = pl.program_id(1)
    tm = x_ref.shape[0]

    @pl.when(j == 0)
    def _():
        def body(r, carry):
            rows = pl.ds(pl.multiple_of(r * LN_ROWS, LN_ROWS), LN_ROWS)
            h_scr[rows, :] = _layer_norm(x_ref[rows, :], g_ref[...], b_ref[...]).astype(BF16)
            return carry
        lax.fori_loop(0, tm // LN_ROWS, body, 0)

    acc = jnp.dot(h_scr[...], w_ref[...].astype(BF16), preferred_element_type=F32)

    @pl.when(j == 0)
    def _():
        pool_ref[...] = acc

    @pl.when(j == 1)
    def _():
        qkv_ref[...] = (acc * q_scale).astype(BF16)

    @pl.when(j > 1)
    def _():
        qkv_ref[...] = acc.astype(BF16)


def _in_proj(x2, g, b, w_in, pool_width, q_scale):
    m, d = x2.shape
    n = w_in.shape[1]
    tm, tn = INPROJ_TM, INPROJ_TN
    assert pool_width == tn and n == 4 * tn and m % tm == 0
    return pl.pallas_call(
        functools.partial(_inproj_kernel, q_scale=q_scale),
        grid=(m // tm, n // tn),
        in_specs=[
            pl.BlockSpec((tm, d), lambda i, j: (i, 0)),
            pl.BlockSpec((1, d), lambda i, j: (0, 0)),
            pl.BlockSpec((1, d), lambda i, j: (0, 0)),
            pl.BlockSpec((d, tn), lambda i, j: (0, j)),
        ],
        out_specs=[
            pl.BlockSpec((tm, tn), lambda i, j: (i, 0)),
            pl.BlockSpec((tm, tn), lambda i, j: (i, jnp.maximum(j - 1, 0))),
        ],
        out_shape=[
            jax.ShapeDtypeStruct((m, pool_width), F32),
            jax.ShapeDtypeStruct((m, n - pool_width), BF16),
        ],
        scratch_shapes=[pltpu.VMEM((tm, d), BF16)],
        compiler_params=pltpu.CompilerParams(
            dimension_semantics=("arbitrary", "arbitrary"),
            vmem_limit_bytes=VMEM_LIMIT_BYTES),
        name="in_proj",
    )(x2, g, b, w_in)


def _attn_kernel(*refs, n_cast):
    q_ref, k_ref, v_ref, cum_ref = refs[:4]
    cast_in = refs[4:4 + n_cast]
    o_ref = refs[4 + n_cast]
    cast_out = refs[5 + n_cast:5 + 2 * n_cast]
    carry_scr, acc_scr, w_scr, a_scr = refs[5 + 2 * n_cast:]
    for src, dst in zip(cast_in, cast_out):
        dst[...] = src[...].astype(dst.dtype)
    qi = pl.program_id(2)
    t = ATTN_T
    nh = q_ref.shape[1] // HEAD_DIM
    skew = ATTN_SKEW
    heads = range(nh)
    cols = [slice(h * HEAD_DIM, (h + 1) * HEAD_DIM) for h in heads]

    def key_rows(kt):
        return pl.ds(pl.multiple_of(kt * t, t), t)

    def scores(kt, h):
        return lax.dot_general(q_ref[:, cols[h]], k_ref[key_rows(kt), cols[h]],
                               (((1,), (1,)), ((), ())),
                               preferred_element_type=F32)

    def deferred_values(kt, j):
        h = nh - skew + j
        acc_scr[h] += jnp.dot(a_scr[j], v_ref[key_rows(kt), cols[h]],
                              preferred_element_type=F32)

    def sweep(kts, first):
        n_items = len(kts) * nh
        item = lambda n: (kts[n // nh], n % nh)
        kt_after = jnp.maximum(kts[-1] - 1, 0)
        if first:
            mask = (lax.broadcasted_iota(jnp.int32, (t, t), 1)
                    < lax.broadcasted_iota(jnp.int32, (t, t), 0))

        def softplus_cumsum(w):
            sp = jnp.maximum(
                w, jnp.log(1.0 + jnp.exp2(jnp.minimum(w, EXP2_CLAMP))) * LOG2_E)
            if first:
                sp = jnp.where(mask, sp, 0.0)
            return jnp.dot(sp.astype(BF16), cum_ref[...], preferred_element_type=F32)

        def weights_values(n, w, cs):
            kt, h = item(n)
            total = jnp.broadcast_to(cs[:, 0:1], (t, HEAD_DIM))
            if first:
                a = jnp.where(mask, jnp.exp2(w - cs), 0.0)
            else:
                carry = carry_scr[h]
                a = jnp.exp2(w - cs - jnp.concatenate([carry] * (t // HEAD_DIM), axis=1))
            carry_scr[h] = total if first else carry + total
            if n >= n_items - skew:
                a_scr[n - (n_items - skew)] = a.astype(BF16)
                if first:
                    acc_scr[h] = jnp.zeros((t, HEAD_DIM), F32)
                return
            av = jnp.dot(a.astype(BF16), v_ref[key_rows(kt), cols[h]],
                         preferred_element_type=F32)
            if first:
                acc_scr[h] = av
            else:
                acc_scr[h] += av

        w, cs, w_after = {}, {}, {}
        for j in range(skew):
            w[j] = scores(*item(j)) if first else w_scr[j]
        for n in range(n_items):
            if n + skew < n_items:
                w[n + skew] = scores(*item(n + skew))
            else:
                w_after[n + skew - n_items] = scores(kt_after, n + skew - n_items)
            cs[n] = softplus_cumsum(w[n])
            if n >= ATTN_LAG:
                weights_values(n - ATTN_LAG, w.pop(n - ATTN_LAG), cs.pop(n - ATTN_LAG))
            if n < skew and not first:
                deferred_values(kts[0] + 1, n)
        for n in range(n_items - ATTN_LAG, n_items):
            weights_values(n, w.pop(n), cs.pop(n))
        for j in range(skew):
            w_scr[j] = w_after[j]

    sweep([qi], True)

    def pair_body(n, _):
        kt = qi - 1 - 2 * n
        sweep([kt, kt - 1], False)
        return 0

    lax.fori_loop(0, qi // 2, pair_body, 0)

    @pl.when(qi % 2 == 1)
    def _():
        sweep([qi * 0], False)

    for j in range(skew):
        deferred_values(0, j)
    for h in heads:
        o_ref[:, cols[h]] = acc_scr[h].astype(o_ref.dtype)


def _cumsum_matrix():
    j = np.arange(ATTN_T)[:, None]
    s = np.arange(ATTN_T)[None, :]
    return jnp.asarray(j >= s, dtype=BF16)


def _sb_attention(qkv, batch, seq, heads, weights_f32):
    m = qkv.shape[0]
    nh = ATTN_HEADS
    t = ATTN_T
    assert heads % nh == 0 and seq % t == 0
    groups = heads // nh
    nq = seq // t
    width = nh * HEAD_DIM
    cum = _cumsum_matrix()
    steps = batch * groups * nq
    step = lambda b, g, i: ((b * groups + g) * nq + i, 0)
    slabs = []
    for wgt in weights_f32:
        rows, ncol = wgt.shape
        assert rows % (steps * 16) == 0
        slabs.append(pl.BlockSpec((rows // steps, ncol), step))
    out = pl.pallas_call(
        functools.partial(_attn_kernel, n_cast=len(weights_f32)),
        grid=(batch, groups, nq),
        in_specs=[
            pl.BlockSpec((t, width), lambda b, g, i: (b * nq + i, g)),
            pl.BlockSpec((seq, width), lambda b, g, i: (b, groups + g)),
            pl.BlockSpec((seq, width), lambda b, g, i: (b, 2 * groups + g)),
            pl.BlockSpec(cum.shape, lambda b, g, i: (0, 0)),
        ] + slabs,
        out_specs=[pl.BlockSpec((t, width), lambda b, g, i: (b * nq + i, g))] + slabs,
        out_shape=[jax.ShapeDtypeStruct((m, heads * HEAD_DIM), BF16)]
        + [jax.ShapeDtypeStruct(wgt.shape, BF16) for wgt in weights_f32],
        scratch_shapes=[pltpu.VMEM((nh, t, HEAD_DIM), F32), pltpu.VMEM((nh, t, HEAD_DIM), F32),
                        pltpu.VMEM((ATTN_SKEW, t, t), F32), pltpu.VMEM((ATTN_SKEW, t, t), BF16)],
        compiler_params=pltpu.CompilerParams(
            dimension_semantics=("arbitrary", "arbitrary", "arbitrary"),
            vmem_limit_bytes=VMEM_LIMIT_BYTES),
        name="sb_attn",
    )(qkv, qkv, qkv, cum, *weights_f32)
    return out[0], out[1:]


def _mix_kernel(x_ref, up_ref, halo_ref, ysb_ref, wout_ref, wpool_ref, pscale_ref,
                ling_ref, linb_ref, l1g_ref, l1b_ref, h1_ref, h1b_ref, *, seq, alpha):
    i = pl.program_id(0)
    tm = x_ref.shape[0]
    gw = wpool_ref.shape[1]
    blk_in_seq = i % (seq // tm)
    halo = jnp.where(blk_in_seq == 0, 0.0, halo_ref[...])
    u = up_ref[...]
    ext = jnp.concatenate([halo, u], axis=0)
    t = blk_in_seq * tm + lax.broadcasted_iota(jnp.int32, (tm, 1), 0)
    parts = []
    for g, w in enumerate(POOL_WINDOWS):
        cols = slice(g * gw, (g + 1) * gw)
        s = ext[:, cols]
        d = 1
        while d < w:
            s = s + pltpu.roll(s, d, axis=0)
            d *= 2
        s = s[POOL_HALO:, :]
        count = jnp.minimum(t + 1, w).astype(F32)
        y = s / count - u[:, cols]
        yp = jnp.dot(y.astype(BF16), wpool_ref[g], preferred_element_type=F32)
        parts.append((yp * pscale_ref[:, cols]).astype(BF16))
    mix_in = jnp.concatenate(parts + [ysb_ref[...]], axis=1)
    mix = jnp.dot(mix_in, wout_ref[...], preferred_element_type=F32)
    h = _layer_norm(x_ref[...], ling_ref[...], linb_ref[...])
    h1 = _layer_norm(alpha * h + mix, l1g_ref[...], l1b_ref[...])
    h1_ref[...] = h1
    h1b_ref[...] = h1.astype(BF16)


def _mix_ln1(x2, u_pool, y_sb, w_out_bf16, w_pool_bf16, pool_scale, ln_in_g, ln_in_b,
             ln1_g, ln1_b, seq, alpha):
    m, d = x2.shape
    pw = u_pool.shape[1]
    sw = y_sb.shape[1]
    tm = MIX_TM
    assert seq % tm == 0 and tm % POOL_HALO == 0 and max(POOL_WINDOWS) <= POOL_HALO
    halo_blocks = tm // POOL_HALO
    const = lambda i: (0, 0)
    return pl.pallas_call(
        functools.partial(_mix_kernel, seq=seq, alpha=alpha),
        grid=(m // tm,),
        in_specs=[
            pl.BlockSpec((tm, d), lambda i: (i, 0)),
            pl.BlockSpec((tm, pw), lambda i: (i, 0)),
            pl.BlockSpec((POOL_HALO, pw), lambda i: (jnp.maximum(i * halo_blocks - 1, 0), 0)),
            pl.BlockSpec((tm, sw), lambda i: (i, 0)),
            pl.BlockSpec(w_out_bf16.shape, const, pipeline_mode=pl.Buffered(1)),
            pl.BlockSpec(w_pool_bf16.shape, lambda i: (0, 0, 0)),
            pl.BlockSpec((1, pw), const),
            pl.BlockSpec((1, d), const),
            pl.BlockSpec((1, d), const),
            pl.BlockSpec((1, d), const),
            pl.BlockSpec((1, d), const),
        ],
        out_specs=[pl.BlockSpec((tm, d), lambda i: (i, 0))] * 2,
        out_shape=[jax.ShapeDtypeStruct((m, d), F32), jax.ShapeDtypeStruct((m, d), BF16)],
        compiler_params=pltpu.CompilerParams(
            dimension_semantics=("arbitrary",),
            vmem_limit_bytes=VMEM_LIMIT_BYTES),
        name="mix_ln1",
    )(x2, u_pool, u_pool, y_sb, w_out_bf16, w_pool_bf16, pool_scale,
      ln_in_g, ln_in_b, ln1_g, ln1_b)


def _ffn_kernel(h1_ref, h1b_ref, w1_ref, b1_ref, w2_ref, b2_ref, g_ref, b_ref, o_ref,
                acc_scr, *, alpha):
    i = pl.program_id(0)
    f = pl.program_id(1)
    last_i = pl.num_programs(0) - 1
    last_f = pl.num_programs(1) - 1

    def mlp_chunk():
        t = jnp.dot(h1b_ref[...], w1_ref[...], preferred_element_type=F32) + b1_ref[...]
        t = jnp.maximum(t, 0.0)
        return jnp.dot((t * t).astype(BF16), w2_ref[...], preferred_element_type=F32)

    def norm():
        return _layer_norm(acc_scr[...], g_ref[...], b_ref[...])

    @pl.when((f == 0) & (i == 0))
    def _():
        acc_scr[...] = mlp_chunk()

    @pl.when((f == 0) & (i > 0))
    def _():
        o_ref[...] = norm()
        acc_scr[...] = mlp_chunk()

    @pl.when((f > 0) & (f < last_f))
    def _():
        acc_scr[...] += mlp_chunk()

    @pl.when(f == last_f)
    def _():
        acc_scr[...] = alpha * h1_ref[...] + (acc_scr[...] + mlp_chunk() + b2_ref[...])

    @pl.when((f == last_f) & (i == last_i))
    def _():
        o_ref[...] = norm()


def _ffn_ln2(h1, h1_bf16, w1_bf16, b1, w2_bf16, b2, g, b, alpha):
    m, d = h1.shape
    dff = w1_bf16.shape[1]
    tm, tf = FFN_TM, FFN_TF
    assert m % tm == 0 and dff % tf == 0 and dff // tf >= 2
    n_i, n_f = m // tm, dff // tf
    const = lambda i, f: (0, 0)

    def out_block(i, f):
        final = (i == n_i - 1) & (f == n_f - 1)
        return (jnp.where(final, i, jnp.maximum(i - 1, 0)), 0)

    return pl.pallas_call(
        functools.partial(_ffn_kernel, alpha=alpha),
        grid=(n_i, n_f),
        in_specs=[
            pl.BlockSpec((tm, d), lambda i, f: (i, 0)),
            pl.BlockSpec((tm, d), lambda i, f: (i, 0)),
            pl.BlockSpec((d, tf), lambda i, f: (0, f)),
            pl.BlockSpec((1, tf), lambda i, f: (0, f)),
            pl.BlockSpec((tf, d), lambda i, f: (f, 0)),
            pl.BlockSpec((1, d), const),
            pl.BlockSpec((1, d), const),
            pl.BlockSpec((1, d), const),
        ],
        out_specs=pl.BlockSpec((tm, d), out_block),
        out_shape=jax.ShapeDtypeStruct((m, d), F32),
        scratch_shapes=[pltpu.VMEM((tm, d), F32)],
        compiler_params=pltpu.CompilerParams(
            dimension_semantics=("arbitrary", "arbitrary"),
            vmem_limit_bytes=VMEM_LIMIT_BYTES),
        name="ffn_ln2",
    )(h1, h1_bf16, w1_bf16, b1, w2_bf16, b2, g, b)


def kernel(x, ln_in_g, ln_in_b, w_in, w_pool, pool_scale, w_out, ln1_g, ln1_b,
           w_ff1, b_ff1, w_ff2, b_ff2, ln2_g, ln2_b):
    batch, seq, d = x.shape
    depth, groups, gw, _ = w_pool.shape
    assert depth == 1, "single trunk layer"
    pool_width = groups * gw
    sb_width = (w_in.shape[2] - pool_width) // 3
    heads = sb_width // HEAD_DIM
    alpha = float((2.0 * depth) ** 0.25)
    m = batch * seq

    x2 = x.reshape(m, d)
    row = lambda p: p.reshape(1, -1)

    q_scale = float(LOG2_E / np.sqrt(np.float32(HEAD_DIM)))
    u_pool, qkv = _in_proj(x2, row(ln_in_g), row(ln_in_b), w_in[0], pool_width, q_scale)
    y_sb, (w_out_bf16, w_ff1_bf16, w_ff2_bf16) = _sb_attention(
        qkv, batch, seq, heads, [w_out[0], w_ff1[0], w_ff2[0]])
    h1, h1_bf16 = _mix_ln1(x2, u_pool, y_sb, w_out_bf16, w_pool[0].astype(BF16),
                  row(pool_scale[0]), row(ln_in_g), row(ln_in_b), row(ln1_g[0]), row(ln1_b[0]),
                  seq, alpha)
    out = _ffn_ln2(h1, h1_bf16, w_ff1_bf16, row(b_ff1[0]), w_ff2_bf16,
                   row(b_ff2[0]), row(ln2_g[0]), row(ln2_b[0]), alpha)
    return out.reshape(batch, seq, d)
```

```python
import functools

import jax
import jax.numpy as jnp
import numpy as np
from jax import lax
from jax.experimental import pallas as pl
from jax.experimental.pallas import tpu as pltpu

F32 = jnp.float32
BF16 = jnp.bfloat16

LN_EPS = 1e-5
LOG2_E = 1.4426950408889634
EXP2_CLAMP = 100.0
POOL_WINDOWS = (2, 4, 8, 16)
POOL_HALO = 16
HEAD_DIM = 128

VMEM_LIMIT_BYTES = 56 * 1024 * 1024
INPROJ_VMEM_LIMIT_BYTES = 60 * 1024 * 1024

INPROJ_TM = 1024
INPROJ_TN = 1024
ATTN_T = 256
ATTN_HEADS = 8
ATTN_SKEW = 2
ATTN_LAG = 4
MIX_TM = 512
FFN_TM = 512
FFN_TF = 1024
LN_ROWS = 128


def _layer_norm(x, g, b):
    mu = jnp.mean(x, axis=-1, keepdims=True)
    xc = x - mu
    var = jnp.mean(xc * xc, axis=-1, keepdims=True)
    return xc * lax.rsqrt(var + LN_EPS) * g + b


def _inproj_kernel(x_ref, g_ref, b_ref, w_ref, pool_ref, qkv_ref, h_scr, *, q_scale):
    j = pl.program_id(1)
    tm = x_ref.shape[0]

    @pl.when(j == 0)
    def _():
        for r in range(tm // LN_ROWS):
            rows = slice(r * LN_ROWS, (r + 1) * LN_ROWS)
            h_scr[rows, :] = _layer_norm(x_ref[rows, :], g_ref[...], b_ref[...]).astype(BF16)

    acc = jnp.dot(h_scr[...], w_ref[...].astype(BF16), preferred_element_type=F32)

    @pl.when(j == 0)
    def _():
        pool_ref[...] = acc

    @pl.when(j == 1)
    def _():
        qkv_ref[...] = (acc * q_scale).astype(BF16)

    @pl.when(j > 1)
    def _():
        qkv_ref[...] = acc.astype(BF16)


def _in_proj(x2, g, b, w_in, pool_width, q_scale):
    m, d = x2.shape
    n = w_in.shape[1]
    tm, tn = INPROJ_TM, INPROJ_TN
    assert pool_width == tn and n == 4 * tn and m % tm == 0
    return pl.pallas_call(
        functools.partial(_inproj_kernel, q_scale=q_scale),
        grid=(m // tm, n // tn),
        in_specs=[
            pl.BlockSpec((tm, d), lambda i, j: (i, 0)),
            pl.BlockSpec((1, d), lambda i, j: (0, 0)),
            pl.BlockSpec((1, d), lambda i, j: (0, 0)),
            pl.BlockSpec((d, tn), lambda i, j: (0, j)),
        ],
        out_specs=[
            pl.BlockSpec((tm, tn), lambda i, j: (i, 0)),
            pl.BlockSpec((tm, tn), lambda i, j: (i, jnp.maximum(j - 1, 0))),
        ],
        out_shape=[
            jax.ShapeDtypeStruct((m, pool_width), F32),
            jax.ShapeDtypeStruct((m, n - pool_width), BF16),
        ],
        scratch_shapes=[pltpu.VMEM((tm, d), BF16)],
        compiler_params=pltpu.CompilerParams(
            dimension_semantics=("arbitrary", "arbitrary"),
            vmem_limit_bytes=INPROJ_VMEM_LIMIT_BYTES),
        name="in_proj",
    )(x2, g, b, w_in)


def _attn_kernel(*refs, n_cast):
    q_ref, k_ref, v_ref, cum_ref = refs[:4]
    cast_in = refs[4:4 + n_cast]
    o_ref = refs[4 + n_cast]
    cast_out = refs[5 + n_cast:5 + 2 * n_cast]
    carry_scr, acc_scr, w_scr, a_scr = refs[5 + 2 * n_cast:]
    for src, dst in zip(cast_in, cast_out):
        dst[...] = src[...].astype(dst.dtype)
    qi = pl.program_id(2)
    t = ATTN_T
    nh = q_ref.shape[1] // HEAD_DIM
    skew = ATTN_SKEW
    heads = range(nh)
    cols = [slice(h * HEAD_DIM, (h + 1) * HEAD_DIM) for h in heads]

    def key_rows(kt):
        return pl.ds(pl.multiple_of(kt * t, t), t)

    def scores(kt, h):
        return lax.dot_general(q_ref[:, cols[h]], k_ref[key_rows(kt), cols[h]],
                               (((1,), (1,)), ((), ())),
                               preferred_element_type=F32)

    def deferred_values(kt, j):
        h = nh - skew + j
        acc_scr[h] += jnp.dot(a_scr[j], v_ref[key_rows(kt), cols[h]],
                              preferred_element_type=F32)

    def sweep(kts, first):
        n_items = len(kts) * nh
        item = lambda n: (kts[n // nh], n % nh)
        kt_after = jnp.maximum(kts[-1] - 1, 0)
        if first:
            mask = (lax.broadcasted_iota(jnp.int32, (t, t), 1)
                    < lax.broadcasted_iota(jnp.int32, (t, t), 0))

        def on_diagonal(n):
            return first and n < nh

        def softplus_cumsum(n, w):
            sp = jnp.maximum(
                w, jnp.log(1.0 + jnp.exp2(jnp.minimum(w, EXP2_CLAMP))) * LOG2_E)
            if on_diagonal(n):
                sp = jnp.where(mask, sp, 0.0)
            return jnp.dot(sp.astype(BF16), cum_ref[...], preferred_element_type=F32)

        def weights_values(n, w, cs):
            kt, h = item(n)
            diag = on_diagonal(n)
            total = jnp.broadcast_to(cs[:, 0:1], (t, HEAD_DIM))
            if diag:
                a = jnp.where(mask, jnp.exp2(w - cs), 0.0)
            else:
                carry = carry_scr[h]
                a = jnp.exp2(w - cs - jnp.concatenate([carry] * (t // HEAD_DIM), axis=1))
            carry_scr[h] = total if diag else carry + total
            if n >= n_items - skew:
                a_scr[n - (n_items - skew)] = a.astype(BF16)
                if diag:
                    acc_scr[h] = jnp.zeros((t, HEAD_DIM), F32)
                return
            av = jnp.dot(a.astype(BF16), v_ref[key_rows(kt), cols[h]],
                         preferred_element_type=F32)
            if diag:
                acc_scr[h] = av
            else:
                acc_scr[h] += av

        w, cs, w_after = {}, {}, {}
        for j in range(skew):
            w[j] = scores(*item(j)) if first else w_scr[j]
        for n in range(n_items):
            if n + skew < n_items:
                w[n + skew] = scores(*item(n + skew))
            else:
                w_after[n + skew - n_items] = scores(kt_after, n + skew - n_items)
            cs[n] = softplus_cumsum(n, w[n])
            if n >= ATTN_LAG:
                weights_values(n - ATTN_LAG, w.pop(n - ATTN_LAG), cs.pop(n - ATTN_LAG))
            if n < skew and not first:
                deferred_values(kts[0] + 1, n)
        for n in range(n_items - ATTN_LAG, n_items):
            weights_values(n, w.pop(n), cs.pop(n))
        for j in range(skew):
            w_scr[j] = w_after[j]

    @pl.when(qi == 0)
    def _():
        sweep([qi], True)

    @pl.when(qi > 0)
    def _():
        sweep([qi, qi - 1], True)
        rest = qi - 1

        def pair_body(n, _):
            kt = rest - 1 - 2 * n
            sweep([kt, kt - 1], False)
            return 0

        lax.fori_loop(0, rest // 2, pair_body, 0)

        @pl.when(rest % 2 == 1)
        def _():
            sweep([qi * 0], False)

    for j in range(skew):
        deferred_values(0, j)
    for h in heads:
        o_ref[:, cols[h]] = acc_scr[h].astype(o_ref.dtype)


def _cumsum_matrix():
    j = np.arange(ATTN_T)[:, None]
    s = np.arange(ATTN_T)[None, :]
    return jnp.asarray(j >= s, dtype=BF16)


def _sb_attention(qkv, batch, seq, heads, weights_f32):
    m = qkv.shape[0]
    nh = ATTN_HEADS
    t = ATTN_T
    assert heads % nh == 0 and seq % t == 0
    groups = heads // nh
    nq = seq // t
    width = nh * HEAD_DIM
    cum = _cumsum_matrix()
    steps = batch * groups * nq
    step = lambda b, g, i: ((b * groups + g) * nq + i, 0)
    slabs = []
    for wgt in weights_f32:
        rows, ncol = wgt.shape
        assert rows % (steps * 16) == 0
        slabs.append(pl.BlockSpec((rows // steps, ncol), step))
    out = pl.pallas_call(
        functools.partial(_attn_kernel, n_cast=len(weights_f32)),
        grid=(batch, groups, nq),
        in_specs=[
            pl.BlockSpec((t, width), lambda b, g, i: (b * nq + i, g)),
            pl.BlockSpec((seq, width), lambda b, g, i: (b, groups + g)),
            pl.BlockSpec((seq, width), lambda b, g, i: (b, 2 * groups + g)),
            pl.BlockSpec(cum.shape, lambda b, g, i: (0, 0)),
        ] + slabs,
        out_specs=[pl.BlockSpec((t, width), lambda b, g, i: (b * nq + i, g))] + slabs,
        out_shape=[jax.ShapeDtypeStruct((m, heads * HEAD_DIM), BF16)]
        + [jax.ShapeDtypeStruct(wgt.shape, BF16) for wgt in weights_f32],
        scratch_shapes=[pltpu.VMEM((nh, t, HEAD_DIM), F32), pltpu.VMEM((nh, t, HEAD_DIM), F32),
                        pltpu.VMEM((ATTN_SKEW, t, t), F32), pltpu.VMEM((ATTN_SKEW, t, t), BF16)],
        compiler_params=pltpu.CompilerParams(
            dimension_semantics=("arbitrary", "arbitrary", "arbitrary"),
            vmem_limit_bytes=VMEM_LIMIT_BYTES),
        name="sb_attn",
    )(qkv, qkv, qkv, cum, *weights_f32)
    return out[0], out[1:]


def _mix_kernel(x_ref, up_ref, halo_ref, ysb_ref, wout_ref, wpool_ref, pscale_ref,
                ling_ref, linb_ref, l1g_ref, l1b_ref, h1_ref, h1b_ref, *, seq, alpha):
    i = pl.program_id(0)
    tm = x_ref.shape[0]
    gw = wpool_ref.shape[1]
    blk_in_seq = i % (seq // tm)
    halo = jnp.where(blk_in_seq == 0, 0.0, halo_ref[...])
    u = up_ref[...]
    ext = jnp.concatenate([halo, u], axis=0)
    t = blk_in_seq * tm + lax.broadcasted_iota(jnp.int32, (tm, 1), 0)
    parts = []
    for g, w in enumerate(POOL_WINDOWS):
        cols = slice(g * gw, (g + 1) * gw)
        s = ext[:, cols]
        d = 1
        while d < w:
            s = s + pltpu.roll(s, d, axis=0)
            d *= 2
        s = s[POOL_HALO:, :]
        count = jnp.minimum(t + 1, w).astype(F32)
        y = s / count - u[:, cols]
        yp = jnp.dot(y.astype(BF16), wpool_ref[g], preferred_element_type=F32)
        parts.append((yp * pscale_ref[:, cols]).astype(BF16))
    mix_in = jnp.concatenate(parts + [ysb_ref[...]], axis=1)
    mix = jnp.dot(mix_in, wout_ref[...], preferred_element_type=F32)
    h = _layer_norm(x_ref[...], ling_ref[...], linb_ref[...])
    h1 = _layer_norm(alpha * h + mix, l1g_ref[...], l1b_ref[...])
    h1_ref[...] = h1
    h1b_ref[...] = h1.astype(BF16)


def _mix_ln1(x2, u_pool, y_sb, w_out_bf16, w_pool_bf16, pool_scale, ln_in_g, ln_in_b,
             ln1_g, ln1_b, seq, alpha):
    m, d = x2.shape
    pw = u_pool.shape[1]
    sw = y_sb.shape[1]
    tm = MIX_TM
    assert seq % tm == 0 and tm % POOL_HALO == 0 and max(POOL_WINDOWS) <= POOL_HALO
    halo_blocks = tm // POOL_HALO
    const = lambda i: (0, 0)
    return pl.pallas_call(
        functools.partial(_mix_kernel, seq=seq, alpha=alpha),
        grid=(m // tm,),
        in_specs=[
            pl.BlockSpec((tm, d), lambda i: (i, 0)),
            pl.BlockSpec((tm, pw), lambda i: (i, 0)),
            pl.BlockSpec((POOL_HALO, pw), lambda i: (jnp.maximum(i * halo_blocks - 1, 0), 0)),
            pl.BlockSpec((tm, sw), lambda i: (i, 0)),
            pl.BlockSpec(w_out_bf16.shape, const, pipeline_mode=pl.Buffered(1)),
            pl.BlockSpec(w_pool_bf16.shape, lambda i: (0, 0, 0)),
            pl.BlockSpec((1, pw), const),
            pl.BlockSpec((1, d), const),
            pl.BlockSpec((1, d), const),
            pl.BlockSpec((1, d), const),
            pl.BlockSpec((1, d), const),
        ],
        out_specs=[pl.BlockSpec((tm, d), lambda i: (i, 0))] * 2,
        out_shape=[jax.ShapeDtypeStruct((m, d), F32), jax.ShapeDtypeStruct((m, d), BF16)],
        compiler_params=pltpu.CompilerParams(
            dimension_semantics=("arbitrary",),
            vmem_limit_bytes=VMEM_LIMIT_BYTES),
        name="mix_ln1",
    )(x2, u_pool, u_pool, y_sb, w_out_bf16, w_pool_bf16, pool_scale,
      ln_in_g, ln_in_b, ln1_g, ln1_b)


def _ffn_kernel(h1_ref, h1b_ref, w1_ref, b1_ref, w2_ref, b2_ref, g_ref, b_ref, o_ref,
                acc_scr, *, alpha):
    i = pl.program_id(0)
    f = pl.program_id(1)
    last_i = pl.num_programs(0) - 1
    last_f = pl.num_programs(1) - 1

    def mlp_chunk():
        t = jnp.dot(h1b_ref[...], w1_ref[...], preferred_element_type=F32) + b1_ref[...]
        t = jnp.maximum(t, 0.0)
        return jnp.dot((t * t).astype(BF16), w2_ref[...], preferred_element_type=F32)

    def norm():
        return _layer_norm(acc_scr[...], g_ref[...], b_ref[...])

    @pl.when((f == 0) & (i == 0))
    def _():
        acc_scr[...] = mlp_chunk()

    @pl.when((f == 0) & (i > 0))
    def _():
        o_ref[...] = norm()
        acc_scr[...] = mlp_chunk()

    @pl.when((f > 0) & (f < last_f))
    def _():
        acc_scr[...] += mlp_chunk()

    @pl.when(f == last_f)
    def _():
        acc_scr[...] = alpha * h1_ref[...] + (acc_scr[...] + mlp_chunk() + b2_ref[...])

    @pl.when((f == last_f) & (i == last_i))
    def _():
        o_ref[...] = norm()


def _ffn_ln2(h1, h1_bf16, w1_bf16, b1, w2_bf16, b2, g, b, alpha):
    m, d = h1.shape
    dff = w1_bf16.shape[1]
    tm, tf = FFN_TM, FFN_TF
    assert m % tm == 0 and dff % tf == 0 and dff // tf >= 2
    n_i, n_f = m // tm, dff // tf
    const = lambda i, f: (0, 0)

    def out_block(i, f):
        final = (i == n_i - 1) & (f == n_f - 1)
        return (jnp.where(final, i, jnp.maximum(i - 1, 0)), 0)

    return pl.pallas_call(
        functools.partial(_ffn_kernel, alpha=alpha),
        grid=(n_i, n_f),
        in_specs=[
            pl.BlockSpec((tm, d), lambda i, f: (i, 0)),
            pl.BlockSpec((tm, d), lambda i, f: (i, 0)),
            pl.BlockSpec((d, tf), lambda i, f: (0, f)),
            pl.BlockSpec((1, tf), lambda i, f: (0, f)),
            pl.BlockSpec((tf, d), lambda i, f: (f, 0)),
            pl.BlockSpec((1, d), const),
            pl.BlockSpec((1, d), const),
            pl.BlockSpec((1, d), const),
        ],
        out_specs=pl.BlockSpec((tm, d), out_block),
        out_shape=jax.ShapeDtypeStruct((m, d), F32),
        scratch_shapes=[pltpu.VMEM((tm, d), F32)],
        compiler_params=pltpu.CompilerParams(
            dimension_semantics=("arbitrary", "arbitrary"),
            vmem_limit_bytes=VMEM_LIMIT_BYTES),
        name="ffn_ln2",
    )(h1, h1_bf16, w1_bf16, b1, w2_bf16, b2, g, b)


def kernel(x, ln_in_g, ln_in_b, w_in, w_pool, pool_scale, w_out, ln1_g, ln1_b,
           w_ff1, b_ff1, w_ff2, b_ff2, ln2_g, ln2_b):
    batch, seq, d = x.shape
    depth, groups, gw, _ = w_pool.shape
    assert depth == 1, "single trunk layer"
    pool_width = groups * gw
    sb_width = (w_in.shape[2] - pool_width) // 3
    heads = sb_width // HEAD_DIM
    alpha = float((2.0 * depth) ** 0.25)
    m = batch * seq

    x2 = x.reshape(m, d)
    row = lambda p: p.reshape(1, -1)

    q_scale = float(LOG2_E / np.sqrt(np.float32(HEAD_DIM)))
    u_pool, qkv = _in_proj(x2, row(ln_in_g), row(ln_in_b), w_in[0], pool_width, q_scale)
    y_sb, (w_out_bf16, w_ff1_bf16, w_ff2_bf16) = _sb_attention(
        qkv, batch, seq, heads, [w_out[0], w_ff1[0], w_ff2[0]])
    h1, h1_bf16 = _mix_ln1(x2, u_pool, y_sb, w_out_bf16, w_pool[0].astype(BF16),
                  row(pool_scale[0]), row(ln_in_g), row(ln_in_b), row(ln1_g[0]), row(ln1_b[0]),
                  seq, alpha)
    out = _ffn_ln2(h1, h1_bf16, w_ff1_bf16, row(b_ff1[0]), w_ff2_bf16,
                   row(b_ff2[0]), row(ln2_g[0]), row(ln2_b[0]), alpha)
    return out.reshape(batch, seq, d)
```

```python
import functools

import jax
import jax.numpy as jnp
import numpy as np
from jax import lax
from jax.experimental import pallas as pl
from jax.experimental.pallas import tpu as pltpu

F32 = jnp.float32
BF16 = jnp.bfloat16

LN_EPS = 1e-5
LOG2_E = 1.4426950408889634
EXP2_CLAMP = 100.0
POOL_WINDOWS = (2, 4, 8, 16)
POOL_HALO = 16
HEAD_DIM = 128

VMEM_LIMIT_BYTES = 56 * 1024 * 1024

INPROJ_TM = 1024
INPROJ_TN = 1024
ATTN_T = 256
ATTN_HEADS = 8
ATTN_SKEW = 2
ATTN_LAG = 4
MIX_TM = 512
FFN_TM = 512
FFN_TF = 1024
LN_ROWS = 128


def _layer_norm(x, g, b):
    mu = jnp.mean(x, axis=-1, keepdims=True)
    xc = x - mu
    var = jnp.mean(xc * xc, axis=-1, keepdims=True)
    return xc * lax.rsqrt(var + LN_EPS) * g + b


def _inproj_kernel(x_hbm, g_ref, b_ref, w_ref, pool_ref, qkv_ref, x_scr, h_scr, sem, *, q_scale):
    i = pl.program_id(0)
    j = pl.program_id(1)
    n_i = pl.num_programs(0)
    last_j = pl.num_programs(1) - 1
    tm = x_scr.shape[0]

    def x_copy(blk):
        return pltpu.make_async_copy(x_hbm.at[pl.ds(blk * tm, tm), :], x_scr, sem)

    def normalise(slot):
        for r in range(tm // LN_ROWS):
            rows = slice(r * LN_ROWS, (r + 1) * LN_ROWS)
            h_scr[slot, rows, :] = _layer_norm(
                x_scr[rows, :], g_ref[...], b_ref[...]).astype(BF16)

    def project(h):
        return jnp.dot(h, w_ref[...].astype(BF16), preferred_element_type=F32)

    @pl.when((i == 0) & (j == 0))
    def _():
        x_copy(0).start()
        x_copy(0).wait()
        normalise(0)

        @pl.when(n_i > 1)
        def _():
            x_copy(1).start()

    @pl.when(j < last_j)
    def _():
        acc = project(h_scr[i % 2])

        @pl.when(j == 0)
        def _():
            pool_ref[...] = acc

        @pl.when(j == 1)
        def _():
            qkv_ref[...] = (acc * q_scale).astype(BF16)

        @pl.when(j > 1)
        def _():
            qkv_ref[...] = acc.astype(BF16)

    def last_step_and_next_norm(slot):
        x_copy(i + 1).wait()
        acc = project(h_scr[slot])
        normalise(1 - slot)
        qkv_ref[...] = acc.astype(BF16)

        @pl.when(i + 2 < n_i)
        def _():
            x_copy(i + 2).start()

    for parity in (0, 1):
        @pl.when((j == last_j) & (i < n_i - 1) & (i % 2 == parity))
        def _():
            last_step_and_next_norm(parity)

    @pl.when((j == last_j) & (i == n_i - 1))
    def _():
        qkv_ref[...] = project(h_scr[i % 2]).astype(BF16)


def _in_proj(x2, g, b, w_in, pool_width, q_scale):
    m, d = x2.shape
    n = w_in.shape[1]
    tm, tn = INPROJ_TM, INPROJ_TN
    assert pool_width == tn and n == 4 * tn and m % tm == 0
    return pl.pallas_call(
        functools.partial(_inproj_kernel, q_scale=q_scale),
        grid=(m // tm, n // tn),
        in_specs=[
            pl.BlockSpec(memory_space=pl.ANY),
            pl.BlockSpec((1, d), lambda i, j: (0, 0)),
            pl.BlockSpec((1, d), lambda i, j: (0, 0)),
            pl.BlockSpec((d, tn), lambda i, j: (0, j)),
        ],
        out_specs=[
            pl.BlockSpec((tm, tn), lambda i, j: (i, 0)),
            pl.BlockSpec((tm, tn), lambda i, j: (i, jnp.maximum(j - 1, 0))),
        ],
        out_shape=[
            jax.ShapeDtypeStruct((m, pool_width), F32),
            jax.ShapeDtypeStruct((m, n - pool_width), BF16),
        ],
        scratch_shapes=[pltpu.VMEM((tm, d), F32), pltpu.VMEM((2, tm, d), BF16),
                        pltpu.SemaphoreType.DMA(())],
        compiler_params=pltpu.CompilerParams(
            dimension_semantics=("arbitrary", "arbitrary"),
            vmem_limit_bytes=VMEM_LIMIT_BYTES),
        name="in_proj",
    )(x2, g, b, w_in)


def _attn_kernel(*refs, n_cast):
    q_ref, k_ref, v_ref, cum_ref = refs[:4]
    cast_in = refs[4:4 + n_cast]
    o_ref = refs[4 + n_cast]
    cast_out = refs[5 + n_cast:5 + 2 * n_cast]
    carry_scr, acc_scr, w_scr, a_scr = refs[5 + 2 * n_cast:]
    for src, dst in zip(cast_in, cast_out):
        dst[...] = src[...].astype(dst.dtype)
    qi = pl.program_id(2)
    t = ATTN_T
    nh = q_ref.shape[1] // HEAD_DIM
    skew = ATTN_SKEW
    heads = range(nh)
    cols = [slice(h * HEAD_DIM, (h + 1) * HEAD_DIM) for h in heads]

    def key_rows(kt):
        return pl.ds(pl.multiple_of(kt * t, t), t)

    def scores(kt, h):
        return lax.dot_general(q_ref[:, cols[h]], k_ref[key_rows(kt), cols[h]],
                               (((1,), (1,)), ((), ())),
                               preferred_element_type=F32)

    def deferred_values(kt, j):
        h = nh - skew + j
        acc_scr[h] += jnp.dot(a_scr[j], v_ref[key_rows(kt), cols[h]],
                              preferred_element_type=F32)

    def sweep(kts, first):
        n_items = len(kts) * nh
        item = lambda n: (kts[n // nh], n % nh)
        kt_after = jnp.maximum(kts[-1] - 1, 0)
        if first:
            mask = (lax.broadcasted_iota(jnp.int32, (t, t), 1)
                    < lax.broadcasted_iota(jnp.int32, (t, t), 0))

        def on_diagonal(n):
            return first and n < nh

        def softplus_cumsum(n, w):
            sp = jnp.maximum(
                w, jnp.log(1.0 + jnp.exp2(jnp.minimum(w, EXP2_CLAMP))) * LOG2_E)
            if on_diagonal(n):
                sp = jnp.where(mask, sp, 0.0)
            return jnp.dot(sp.astype(BF16), cum_ref[...], preferred_element_type=F32)

        def weights_values(n, w, cs):
            kt, h = item(n)
            diag = on_diagonal(n)
            total = jnp.broadcast_to(cs[:, 0:1], (t, HEAD_DIM))
            if diag:
                a = jnp.where(mask, jnp.exp2(w - cs), 0.0)
            else:
                carry = carry_scr[h]
                a = jnp.exp2(w - cs - jnp.concatenate([carry] * (t // HEAD_DIM), axis=1))
            carry_scr[h] = total if diag else carry + total
            if n >= n_items - skew:
                a_scr[n - (n_items - skew)] = a.astype(BF16)
                if diag:
                    acc_scr[h] = jnp.zeros((t, HEAD_DIM), F32)
                return
            av = jnp.dot(a.astype(BF16), v_ref[key_rows(kt), cols[h]],
                         preferred_element_type=F32)
            if diag:
                acc_scr[h] = av
            else:
                acc_scr[h] += av

        w, cs, w_after = {}, {}, {}
        for j in range(skew):
            w[j] = scores(*item(j)) if first else w_scr[j]
        for n in range(n_items):
            if n + skew < n_items:
                w[n + skew] = scores(*item(n + skew))
            else:
                w_after[n + skew - n_items] = scores(kt_after, n + skew - n_items)
            cs[n] = softplus_cumsum(n, w[n])
            if n >= ATTN_LAG:
                weights_values(n - ATTN_LAG, w.pop(n - ATTN_LAG), cs.pop(n - ATTN_LAG))
            if n < skew and not first:
                deferred_values(kts[0] + 1, n)
        for n in range(n_items - ATTN_LAG, n_items):
            weights_values(n, w.pop(n), cs.pop(n))
        for j in range(skew):
            w_scr[j] = w_after[j]

    @pl.when(qi == 0)
    def _():
        sweep([qi], True)

    @pl.when(qi > 0)
    def _():
        sweep([qi, qi - 1], True)
        rest = qi - 1

        def pair_body(n, _):
            kt = rest - 1 - 2 * n
            sweep([kt, kt - 1], False)
            return 0

        lax.fori_loop(0, rest // 2, pair_body, 0)

        @pl.when(rest % 2 == 1)
        def _():
            sweep([qi * 0], False)

    for j in range(skew):
        deferred_values(0, j)
    for h in heads:
        o_ref[:, cols[h]] = acc_scr[h].astype(o_ref.dtype)


def _cumsum_matrix():
    j = np.arange(ATTN_T)[:, None]
    s = np.arange(ATTN_T)[None, :]
    return jnp.asarray(j >= s, dtype=BF16)


def _sb_attention(qkv, batch, seq, heads, weights_f32):
    m = qkv.shape[0]
    nh = ATTN_HEADS
    t = ATTN_T
    assert heads % nh == 0 and seq % t == 0
    groups = heads // nh
    nq = seq // t
    width = nh * HEAD_DIM
    cum = _cumsum_matrix()
    steps = batch * groups * nq
    step = lambda b, g, i: ((b * groups + g) * nq + i, 0)
    slabs = []
    for wgt in weights_f32:
        rows, ncol = wgt.shape
        assert rows % (steps * 16) == 0
        slabs.append(pl.BlockSpec((rows // steps, ncol), step))
    out = pl.pallas_call(
        functools.partial(_attn_kernel, n_cast=len(weights_f32)),
        grid=(batch, groups, nq),
        in_specs=[
            pl.BlockSpec((t, width), lambda b, g, i: (b * nq + i, g)),
            pl.BlockSpec((seq, width), lambda b, g, i: (b, groups + g)),
            pl.BlockSpec((seq, width), lambda b, g, i: (b, 2 * groups + g)),
            pl.BlockSpec(cum.shape, lambda b, g, i: (0, 0)),
        ] + slabs,
        out_specs=[pl.BlockSpec((t, width), lambda b, g, i: (b * nq + i, g))] + slabs,
        out_shape=[jax.ShapeDtypeStruct((m, heads * HEAD_DIM), BF16)]
        + [jax.ShapeDtypeStruct(wgt.shape, BF16) for wgt in weights_f32],
        scratch_shapes=[pltpu.VMEM((nh, t, HEAD_DIM), F32), pltpu.VMEM((nh, t, HEAD_DIM), F32),
                        pltpu.VMEM((ATTN_SKEW, t, t), F32), pltpu.VMEM((ATTN_SKEW, t, t), BF16)],
        compiler_params=pltpu.CompilerParams(
            dimension_semantics=("arbitrary", "arbitrary", "arbitrary"),
            vmem_limit_bytes=VMEM_LIMIT_BYTES),
        name="sb_attn",
    )(qkv, qkv, qkv, cum, *weights_f32)
    return out[0], out[1:]


def _mix_kernel(x_ref, up_ref, halo_ref, ysb_ref, wout_ref, wpool_ref, pscale_ref,
                ling_ref, linb_ref, l1g_ref, l1b_ref, h1_ref, h1b_ref, *, seq, alpha):
    i = pl.program_id(0)
    tm = x_ref.shape[0]
    gw = wpool_ref.shape[1]
    blk_in_seq = i % (seq // tm)
    halo = jnp.where(blk_in_seq == 0, 0.0, halo_ref[...])
    u = up_ref[...]
    ext = jnp.concatenate([halo, u], axis=0)
    t = blk_in_seq * tm + lax.broadcasted_iota(jnp.int32, (tm, 1), 0)
    parts = []
    for g, w in enumerate(POOL_WINDOWS):
        cols = slice(g * gw, (g + 1) * gw)
        s = ext[:, cols]
        d = 1
        while d < w:
            s = s + pltpu.roll(s, d, axis=0)
            d *= 2
        s = s[POOL_HALO:, :]
        count = jnp.minimum(t + 1, w).astype(F32)
        y = s / count - u[:, cols]
        yp = jnp.dot(y.astype(BF16), wpool_ref[g], preferred_element_type=F32)
        parts.append((yp * pscale_ref[:, cols]).astype(BF16))
    mix_in = jnp.concatenate(parts + [ysb_ref[...]], axis=1)
    mix = jnp.dot(mix_in, wout_ref[...], preferred_element_type=F32)
    h = _layer_norm(x_ref[...], ling_ref[...], linb_ref[...])
    h1 = _layer_norm(alpha * h + mix, l1g_ref[...], l1b_ref[...])
    h1_ref[...] = h1
    h1b_ref[...] = h1.astype(BF16)


def _mix_ln1(x2, u_pool, y_sb, w_out_bf16, w_pool_bf16, pool_scale, ln_in_g, ln_in_b,
             ln1_g, ln1_b, seq, alpha):
    m, d = x2.shape
    pw = u_pool.shape[1]
    sw = y_sb.shape[1]
    tm = MIX_TM
    assert seq % tm == 0 and tm % POOL_HALO == 0 and max(POOL_WINDOWS) <= POOL_HALO
    halo_blocks = tm // POOL_HALO
    const = lambda i: (0, 0)
    return pl.pallas_call(
        functools.partial(_mix_kernel, seq=seq, alpha=alpha),
        grid=(m // tm,),
        in_specs=[
            pl.BlockSpec((tm, d), lambda i: (i, 0)),
            pl.BlockSpec((tm, pw), lambda i: (i, 0)),
            pl.BlockSpec((POOL_HALO, pw), lambda i: (jnp.maximum(i * halo_blocks - 1, 0), 0)),
            pl.BlockSpec((tm, sw), lambda i: (i, 0)),
            pl.BlockSpec(w_out_bf16.shape, const, pipeline_mode=pl.Buffered(1)),
            pl.BlockSpec(w_pool_bf16.shape, lambda i: (0, 0, 0)),
            pl.BlockSpec((1, pw), const),
            pl.BlockSpec((1, d), const),
            pl.BlockSpec((1, d), const),
            pl.BlockSpec((1, d), const),
            pl.BlockSpec((1, d), const),
        ],
        out_specs=[pl.BlockSpec((tm, d), lambda i: (i, 0))] * 2,
        out_shape=[jax.ShapeDtypeStruct((m, d), F32), jax.ShapeDtypeStruct((m, d), BF16)],
        compiler_params=pltpu.CompilerParams(
            dimension_semantics=("arbitrary",),
            vmem_limit_bytes=VMEM_LIMIT_BYTES),
        name="mix_ln1",
    )(x2, u_pool, u_pool, y_sb, w_out_bf16, w_pool_bf16, pool_scale,
      ln_in_g, ln_in_b, ln1_g, ln1_b)


def _ffn_kernel(h1_ref, h1b_ref, w1_ref, b1_ref, w2_ref, b2_ref, g_ref, b_ref, o_ref,
                acc_scr, *, alpha):
    i = pl.program_id(0)
    f = pl.program_id(1)
    last_i = pl.num_programs(0) - 1
    last_f = pl.num_programs(1) - 1

    def mlp_chunk():
        t = jnp.dot(h1b_ref[...], w1_ref[...], preferred_element_type=F32) + b1_ref[...]
        t = jnp.maximum(t, 0.0)
        return jnp.dot((t * t).astype(BF16), w2_ref[...], preferred_element_type=F32)

    def norm():
        return _layer_norm(acc_scr[...], g_ref[...], b_ref[...])

    @pl.when((f == 0) & (i == 0))
    def _():
        acc_scr[...] = mlp_chunk()

    @pl.when((f == 0) & (i > 0))
    def _():
        o_ref[...] = norm()
        acc_scr[...] = mlp_chunk()

    @pl.when((f > 0) & (f < last_f))
    def _():
        acc_scr[...] += mlp_chunk()

    @pl.when(f == last_f)
    def _():
        acc_scr[...] = alpha * h1_ref[...] + (acc_scr[...] + mlp_chunk() + b2_ref[...])

    @pl.when((f == last_f) & (i == last_i))
    def _():
        o_ref[...] = norm()


def _ffn_ln2(h1, h1_bf16, w1_bf16, b1, w2_bf16, b2, g, b, alpha):
    m, d = h1.shape
    dff = w1_bf16.shape[1]
    tm, tf = FFN_TM, FFN_TF
    assert m % tm == 0 and dff % tf == 0 and dff // tf >= 2
    n_i, n_f = m // tm, dff // tf
    const = lambda i, f: (0, 0)

    def out_block(i, f):
        final = (i == n_i - 1) & (f == n_f - 1)
        return (jnp.where(final, i, jnp.maximum(i - 1, 0)), 0)

    return pl.pallas_call(
        functools.partial(_ffn_kernel, alpha=alpha),
        grid=(n_i, n_f),
        in_specs=[
            pl.BlockSpec((tm, d), lambda i, f: (i, 0)),
            pl.BlockSpec((tm, d), lambda i, f: (i, 0)),
            pl.BlockSpec((d, tf), lambda i, f: (0, f)),
            pl.BlockSpec((1, tf), lambda i, f: (0, f)),
            pl.BlockSpec((tf, d), lambda i, f: (f, 0)),
            pl.BlockSpec((1, d), const),
            pl.BlockSpec((1, d), const),
            pl.BlockSpec((1, d), const),
        ],
        out_specs=pl.BlockSpec((tm, d), out_block),
        out_shape=jax.ShapeDtypeStruct((m, d), F32),
        scratch_shapes=[pltpu.VMEM((tm, d), F32)],
        compiler_params=pltpu.CompilerParams(
            dimension_semantics=("arbitrary", "arbitrary"),
            vmem_limit_bytes=VMEM_LIMIT_BYTES),
        name="ffn_ln2",
    )(h1, h1_bf16, w1_bf16, b1, w2_bf16, b2, g, b)


def kernel(x, ln_in_g, ln_in_b, w_in, w_pool, pool_scale, w_out, ln1_g, ln1_b,
           w_ff1, b_ff1, w_ff2, b_ff2, ln2_g, ln2_b):
    batch, seq, d = x.shape
    depth, groups, gw, _ = w_pool.shape
    assert depth == 1, "single trunk layer"
    pool_width = groups * gw
    sb_width = (w_in.shape[2] - pool_width) // 3
    heads = sb_width // HEAD_DIM
    alpha = float((2.0 * depth) ** 0.25)
    m = batch * seq

    x2 = x.reshape(m, d)
    row = lambda p: p.reshape(1, -1)

    q_scale = float(LOG2_E / np.sqrt(np.float32(HEAD_DIM)))
    u_pool, qkv = _in_proj(x2, row(ln_in_g), row(ln_in_b), w_in[0], pool_width, q_scale)
    y_sb, (w_out_bf16, w_ff1_bf16, w_ff2_bf16) = _sb_attention(
        qkv, batch, seq, heads, [w_out[0], w_ff1[0], w_ff2[0]])
    h1, h1_bf16 = _mix_ln1(x2, u_pool, y_sb, w_out_bf16, w_pool[0].astype(BF16),
                  row(pool_scale[0]), row(ln_in_g), row(ln_in_b), row(ln1_g[0]), row(ln1_b[0]),
                  seq, alpha)
    out = _ffn_ln2(h1, h1_bf16, w_ff1_bf16, row(b_ff1[0]), w_ff2_bf16,
                   row(b_ff2[0]), row(ln2_g[0]), row(ln2_b[0]), alpha)
    return out.reshape(batch, seq, d)
```

```python
import functools

import jax
import jax.numpy as jnp
import numpy as np
from jax import lax
from jax.experimental import pallas as pl
from jax.experimental.pallas import tpu as pltpu

F32 = jnp.float32
BF16 = jnp.bfloat16

LN_EPS = 1e-5
LOG2_E = 1.4426950408889634
EXP2_CLAMP = 100.0
POOL_WINDOWS = (2, 4, 8, 16)
POOL_HALO = 16
HEAD_DIM = 128

VMEM_LIMIT_BYTES = 56 * 1024 * 1024

INPROJ_TM = 1024
INPROJ_TN = 1024
ATTN_T = 256
ATTN_HEADS = 8
ATTN_SKEW = 2
ATTN_LAG = 4
MIX_TM = 512
FFN_TM = 512
FFN_TF = 1024
LN_ROWS = 128


def _layer_norm(x, g, b):
    mu = jnp.mean(x, axis=-1, keepdims=True)
    xc = x - mu
    var = jnp.mean(xc * xc, axis=-1, keepdims=True)
    return xc * lax.rsqrt(var + LN_EPS) * g + b


def _inproj_kernel(x_hbm, g_ref, b_ref, w_ref, pool_ref, qkv_ref, x_scr, h_scr, sem, *, q_scale):
    i = pl.program_id(0)
    j = pl.program_id(1)
    n_i = pl.num_programs(0)
    last_j = pl.num_programs(1) - 1
    tm = x_scr.shape[0]

    def x_copy(blk):
        return pltpu.make_async_copy(x_hbm.at[pl.ds(blk * tm, tm), :], x_scr, sem)

    def normalise(slot):
        for r in range(tm // LN_ROWS):
            rows = slice(r * LN_ROWS, (r + 1) * LN_ROWS)
            h_scr[slot, rows, :] = _layer_norm(
                x_scr[rows, :], g_ref[...], b_ref[...]).astype(BF16)

    def project(h):
        return jnp.dot(h, w_ref[...].astype(BF16), preferred_element_type=F32)

    @pl.when((i == 0) & (j == 0))
    def _():
        x_copy(0).start()
        x_copy(0).wait()
        normalise(0)

        @pl.when(n_i > 1)
        def _():
            x_copy(1).start()

    @pl.when(j < last_j)
    def _():
        acc = project(h_scr[i % 2])

        @pl.when(j == 0)
        def _():
            pool_ref[...] = acc

        @pl.when(j == 1)
        def _():
            qkv_ref[...] = (acc * q_scale).astype(BF16)

        @pl.when(j > 1)
        def _():
            qkv_ref[...] = acc.astype(BF16)

    def last_step_and_next_norm(slot):
        x_copy(i + 1).wait()
        acc = project(h_scr[slot])
        normalise(1 - slot)
        qkv_ref[...] = acc.astype(BF16)

        @pl.when(i + 2 < n_i)
        def _():
            x_copy(i + 2).start()

    for parity in (0, 1):
        @pl.when((j == last_j) & (i < n_i - 1) & (i % 2 == parity))
        def _():
            last_step_and_next_norm(parity)

    @pl.when((j == last_j) & (i == n_i - 1))
    def _():
        qkv_ref[...] = project(h_scr[i % 2]).astype(BF16)


def _in_proj(x2, g, b, w_in, pool_width, q_scale):
    m, d = x2.shape
    n = w_in.shape[1]
    tm, tn = INPROJ_TM, INPROJ_TN
    assert pool_width == tn and n == 4 * tn and m % tm == 0
    return pl.pallas_call(
        functools.partial(_inproj_kernel, q_scale=q_scale),
        grid=(m // tm, n // tn),
        in_specs=[
            pl.BlockSpec(memory_space=pl.ANY),
            pl.BlockSpec((1, d), lambda i, j: (0, 0)),
            pl.BlockSpec((1, d), lambda i, j: (0, 0)),
            pl.BlockSpec((d, tn), lambda i, j: (0, j)),
        ],
        out_specs=[
            pl.BlockSpec((tm, tn), lambda i, j: (i, 0)),
            pl.BlockSpec((tm, tn), lambda i, j: (i, jnp.maximum(j - 1, 0))),
        ],
        out_shape=[
            jax.ShapeDtypeStruct((m, pool_width), F32),
            jax.ShapeDtypeStruct((m, n - pool_width), BF16),
        ],
        scratch_shapes=[pltpu.VMEM((tm, d), F32), pltpu.VMEM((2, tm, d), BF16),
                        pltpu.SemaphoreType.DMA(())],
        compiler_params=pltpu.CompilerParams(
            dimension_semantics=("arbitrary", "arbitrary"),
            vmem_limit_bytes=VMEM_LIMIT_BYTES),
        name="in_proj",
    )(x2, g, b, w_in)


def _attn_kernel(*refs, n_cast):
    q_ref, k_ref, v_ref = refs[:3]
    cast_in = refs[3:3 + n_cast]
    o_ref = refs[3 + n_cast]
    cast_out = refs[4 + n_cast:4 + 2 * n_cast]
    carry_scr, acc_scr, w_scr, a_scr, cum_scr = refs[4 + 2 * n_cast:]
    cum_scr[...] = jnp.where(
        lax.broadcasted_iota(jnp.int32, cum_scr.shape, 0)
        >= lax.broadcasted_iota(jnp.int32, cum_scr.shape, 1), 1.0, 0.0).astype(BF16)
    for src, dst in zip(cast_in, cast_out):
        dst[...] = src[...].astype(dst.dtype)
    qi = pl.program_id(2)
    t = ATTN_T
    nh = q_ref.shape[1] // HEAD_DIM
    skew = ATTN_SKEW
    heads = range(nh)
    cols = [slice(h * HEAD_DIM, (h + 1) * HEAD_DIM) for h in heads]

    def key_rows(kt):
        return pl.ds(pl.multiple_of(kt * t, t), t)

    def scores(kt, h):
        return lax.dot_general(q_ref[:, cols[h]], k_ref[key_rows(kt), cols[h]],
                               (((1,), (1,)), ((), ())),
                               preferred_element_type=F32)

    def deferred_values(kt, j):
        h = nh - skew + j
        acc_scr[h] += jnp.dot(a_scr[j], v_ref[key_rows(kt), cols[h]],
                              preferred_element_type=F32)

    def sweep(kts, first):
        n_items = len(kts) * nh
        item = lambda n: (kts[n // nh], n % nh)
        kt_after = jnp.maximum(kts[-1] - 1, 0)
        if first:
            mask = (lax.broadcasted_iota(jnp.int32, (t, t), 1)
                    < lax.broadcasted_iota(jnp.int32, (t, t), 0))

        def on_diagonal(n):
            return first and n < nh

        def softplus_cumsum(n, w):
            sp = jnp.maximum(
                w, jnp.log(1.0 + jnp.exp2(jnp.minimum(w, EXP2_CLAMP))) * LOG2_E)
            if on_diagonal(n):
                sp = jnp.where(mask, sp, 0.0)
            return jnp.dot(sp.astype(BF16), cum_scr[...], preferred_element_type=F32)

        def weights_values(n, w, cs):
            kt, h = item(n)
            diag = on_diagonal(n)
            total = jnp.broadcast_to(cs[:, 0:1], (t, HEAD_DIM))
            if diag:
                a = jnp.where(mask, jnp.exp2(w - cs), 0.0)
            else:
                carry = carry_scr[h]
                a = jnp.exp2(w - cs - jnp.concatenate([carry] * (t // HEAD_DIM), axis=1))
            carry_scr[h] = total if diag else carry + total
            if n >= n_items - skew:
                a_scr[n - (n_items - skew)] = a.astype(BF16)
                if diag:
                    acc_scr[h] = jnp.zeros((t, HEAD_DIM), F32)
                return
            av = jnp.dot(a.astype(BF16), v_ref[key_rows(kt), cols[h]],
                         preferred_element_type=F32)
            if diag:
                acc_scr[h] = av
            else:
                acc_scr[h] += av

        w, cs, w_after = {}, {}, {}
        for j in range(skew):
            w[j] = scores(*item(j)) if first else w_scr[j]
        for n in range(n_items):
            if n + skew < n_items:
                w[n + skew] = scores(*item(n + skew))
            else:
                w_after[n + skew - n_items] = scores(kt_after, n + skew - n_items)
            cs[n] = softplus_cumsum(n, w[n])
            if n >= ATTN_LAG:
                weights_values(n - ATTN_LAG, w.pop(n - ATTN_LAG), cs.pop(n - ATTN_LAG))
            if n < skew and not first:
                deferred_values(kts[0] + 1, n)
        for n in range(n_items - ATTN_LAG, n_items):
            weights_values(n, w.pop(n), cs.pop(n))
        for j in range(skew):
            w_scr[j] = w_after[j]

    @pl.when(qi == 0)
    def _():
        sweep([qi], True)

    @pl.when(qi > 0)
    def _():
        sweep([qi, qi - 1], True)
        rest = qi - 1

        def pair_body(n, _):
            kt = rest - 1 - 2 * n
            sweep([kt, kt - 1], False)
            return 0

        lax.fori_loop(0, rest // 2, pair_body, 0)

        @pl.when(rest % 2 == 1)
        def _():
            sweep([qi * 0], False)

    for j in range(skew):
        deferred_values(0, j)
    for h in heads:
        o_ref[:, cols[h]] = acc_scr[h].astype(o_ref.dtype)


def _sb_attention(qkv, batch, seq, heads, weights_f32):
    m = qkv.shape[0]
    nh = ATTN_HEADS
    t = ATTN_T
    assert heads % nh == 0 and seq % t == 0
    groups = heads // nh
    nq = seq // t
    width = nh * HEAD_DIM
    steps = batch * groups * nq
    step = lambda b, g, i: ((b * groups + g) * nq + i, 0)
    slabs = []
    for wgt in weights_f32:
        rows, ncol = wgt.shape
        assert rows % (steps * 16) == 0
        slabs.append(pl.BlockSpec((rows // steps, ncol), step))
    out = pl.pallas_call(
        functools.partial(_attn_kernel, n_cast=len(weights_f32)),
        grid=(batch, groups, nq),
        in_specs=[
            pl.BlockSpec((t, width), lambda b, g, i: (b * nq + i, g)),
            pl.BlockSpec((seq, width), lambda b, g, i: (b, groups + g)),
            pl.BlockSpec((seq, width), lambda b, g, i: (b, 2 * groups + g)),
        ] + slabs,
        out_specs=[pl.BlockSpec((t, width), lambda b, g, i: (b * nq + i, g))] + slabs,
        out_shape=[jax.ShapeDtypeStruct((m, heads * HEAD_DIM), BF16)]
        + [jax.ShapeDtypeStruct(wgt.shape, BF16) for wgt in weights_f32],
        scratch_shapes=[pltpu.VMEM((nh, t, HEAD_DIM), F32), pltpu.VMEM((nh, t, HEAD_DIM), F32),
                        pltpu.VMEM((ATTN_SKEW, t, t), F32), pltpu.VMEM((ATTN_SKEW, t, t), BF16),
                        pltpu.VMEM((t, t), BF16)],
        compiler_params=pltpu.CompilerParams(
            dimension_semantics=("arbitrary", "arbitrary", "arbitrary"),
            vmem_limit_bytes=VMEM_LIMIT_BYTES),
        name="sb_attn",
    )(qkv, qkv, qkv, *weights_f32)
    return out[0], out[1:]


def _mix_kernel(x_ref, up_ref, halo_ref, ysb_ref, wout_ref, wpool_ref, pscale_ref,
                ling_ref, linb_ref, l1g_ref, l1b_ref, h1_ref, h1b_ref, *, seq, alpha):
    i = pl.program_id(0)
    tm = x_ref.shape[0]
    gw = wpool_ref.shape[1]
    blk_in_seq = i % (seq // tm)
    halo = jnp.where(blk_in_seq == 0, 0.0, halo_ref[...])
    u = up_ref[...]
    ext = jnp.concatenate([halo, u], axis=0)
    t = blk_in_seq * tm + lax.broadcasted_iota(jnp.int32, (tm, 1), 0)
    parts = []
    for g, w in enumerate(POOL_WINDOWS):
        cols = slice(g * gw, (g + 1) * gw)
        s = ext[:, cols]
        d = 1
        while d < w:
            s = s + pltpu.roll(s, d, axis=0)
            d *= 2
        s = s[POOL_HALO:, :]
        count = jnp.minimum(t + 1, w).astype(F32)
        y = s / count - u[:, cols]
        yp = jnp.dot(y.astype(BF16), wpool_ref[g], preferred_element_type=F32)
        parts.append((yp * pscale_ref[:, cols]).astype(BF16))
    mix_in = jnp.concatenate(parts + [ysb_ref[...]], axis=1)
    mix = jnp.dot(mix_in, wout_ref[...], preferred_element_type=F32)
    h = _layer_norm(x_ref[...], ling_ref[...], linb_ref[...])
    h1 = _layer_norm(alpha * h + mix, l1g_ref[...], l1b_ref[...])
    h1_ref[...] = h1
    h1b_ref[...] = h1.astype(BF16)


def _mix_ln1(x2, u_pool, y_sb, w_out_bf16, w_pool_bf16, pool_scale, ln_in_g, ln_in_b,
             ln1_g, ln1_b, seq, alpha):
    m, d = x2.shape
    pw = u_pool.shape[1]
    sw = y_sb.shape[1]
    tm = MIX_TM
    assert seq % tm == 0 and tm % POOL_HALO == 0 and max(POOL_WINDOWS) <= POOL_HALO
    halo_blocks = tm // POOL_HALO
    const = lambda i: (0, 0)
    return pl.pallas_call(
        functools.partial(_mix_kernel, seq=seq, alpha=alpha),
        grid=(m // tm,),
        in_specs=[
            pl.BlockSpec((tm, d), lambda i: (i, 0)),
            pl.BlockSpec((tm, pw), lambda i: (i, 0)),
            pl.BlockSpec((POOL_HALO, pw), lambda i: (jnp.maximum(i * halo_blocks - 1, 0), 0)),
            pl.BlockSpec((tm, sw), lambda i: (i, 0)),
            pl.BlockSpec(w_out_bf16.shape, const, pipeline_mode=pl.Buffered(1)),
            pl.BlockSpec(w_pool_bf16.shape, lambda i: (0, 0, 0)),
            pl.BlockSpec((1, pw), const),
            pl.BlockSpec((1, d), const),
            pl.BlockSpec((1, d), const),
            pl.BlockSpec((1, d), const),
            pl.BlockSpec((1, d), const),
        ],
        out_specs=[pl.BlockSpec((tm, d), lambda i: (i, 0))] * 2,
        out_shape=[jax.ShapeDtypeStruct((m, d), F32), jax.ShapeDtypeStruct((m, d), BF16)],
        compiler_params=pltpu.CompilerParams(
            dimension_semantics=("arbitrary",),
            vmem_limit_bytes=VMEM_LIMIT_BYTES),
        name="mix_ln1",
    )(x2, u_pool, u_pool, y_sb, w_out_bf16, w_pool_bf16, pool_scale,
      ln_in_g, ln_in_b, ln1_g, ln1_b)


def _ffn_kernel(h1_ref, h1b_ref, w1_ref, b1_ref, w2_ref, b2_ref, g_ref, b_ref, o_ref,
                acc_scr, *, alpha):
    i = pl.program_id(0)
    f = pl.program_id(1)
    last_i = pl.num_programs(0) - 1
    last_f = pl.num_programs(1) - 1

    def mlp_chunk():
        t = jnp.dot(h1b_ref[...], w1_ref[...], preferred_element_type=F32) + b1_ref[...]
        t = jnp.maximum(t, 0.0)
        return jnp.dot((t * t).astype(BF16), w2_ref[...], preferred_element_type=F32)

    def norm():
        return _layer_norm(acc_scr[...], g_ref[...], b_ref[...])

    @pl.when((f == 0) & (i == 0))
    def _():
        acc_scr[...] = mlp_chunk()

    @pl.when((f == 0) & (i > 0))
    def _():
        o_ref[...] = norm()
        acc_scr[...] = mlp_chunk()

    @pl.when((f > 0) & (f < last_f))
    def _():
        acc_scr[...] += mlp_chunk()

    @pl.when(f == last_f)
    def _():
        acc_scr[...] = alpha * h1_ref[...] + (acc_scr[...] + mlp_chunk() + b2_ref[...])

    @pl.when((f == last_f) & (i == last_i))
    def _():
        o_ref[...] = norm()


def _ffn_ln2(h1, h1_bf16, w1_bf16, b1, w2_bf16, b2, g, b, alpha):
    m, d = h1.shape
    dff = w1_bf16.shape[1]
    tm, tf = FFN_TM, FFN_TF
    assert m % tm == 0 and dff % tf == 0 and dff // tf >= 2
    n_i, n_f = m // tm, dff // tf
    const = lambda i, f: (0, 0)

    def out_block(i, f):
        final = (i == n_i - 1) & (f == n_f - 1)
        return (jnp.where(final, i, jnp.maximum(i - 1, 0)), 0)

    return pl.pallas_call(
        functools.partial(_ffn_kernel, alpha=alpha),
        grid=(n_i, n_f),
        in_specs=[
            pl.BlockSpec((tm, d), lambda i, f: (i, 0)),
            pl.BlockSpec((tm, d), lambda i, f: (i, 0)),
            pl.BlockSpec((d, tf), lambda i, f: (0, f)),
            pl.BlockSpec((1, tf), lambda i, f: (0, f)),
            pl.BlockSpec((tf, d), lambda i, f: (f, 0)),
            pl.BlockSpec((1, d), const),
            pl.BlockSpec((1, d), const),
            pl.BlockSpec((1, d), const),
        ],
        out_specs=pl.BlockSpec((tm, d), out_block),
        out_shape=jax.ShapeDtypeStruct((m, d), F32),
        scratch_shapes=[pltpu.VMEM((tm, d), F32)],
        compiler_params=pltpu.CompilerParams(
            dimension_semantics=("arbitrary", "arbitrary"),
            vmem_limit_bytes=VMEM_LIMIT_BYTES),
        name="ffn_ln2",
    )(h1, h1_bf16, w1_bf16, b1, w2_bf16, b2, g, b)


def kernel(x, ln_in_g, ln_in_b, w_in, w_pool, pool_scale, w_out, ln1_g, ln1_b,
           w_ff1, b_ff1, w_ff2, b_ff2, ln2_g, ln2_b):
    batch, seq, d = x.shape
    depth, groups, gw, _ = w_pool.shape
    assert depth == 1, "single trunk layer"
    pool_width = groups * gw
    sb_width = (w_in.shape[2] - pool_width) // 3
    heads = sb_width // HEAD_DIM
    alpha = float((2.0 * depth) ** 0.25)
    m = batch * seq

    x2 = x.reshape(m, d)
    row = lambda p: p.reshape(1, -1)

    q_scale = float(LOG2_E / np.sqrt(np.float32(HEAD_DIM)))
    u_pool, qkv = _in_proj(x2, row(ln_in_g), row(ln_in_b), w_in[0], pool_width, q_scale)
    y_sb, (w_out_bf16, w_ff1_bf16, w_ff2_bf16, w_pool_bf16) = _sb_attention(
        qkv, batch, seq, heads,
        [w_out[0], w_ff1[0], w_ff2[0], w_pool[0].reshape(groups * gw, gw)])
    h1, h1_bf16 = _mix_ln1(x2, u_pool, y_sb, w_out_bf16, w_pool_bf16.reshape(groups, gw, gw),
                  row(pool_scale[0]), row(ln_in_g), row(ln_in_b), row(ln1_g[0]), row(ln1_b[0]),
                  seq, alpha)
    out = _ffn_ln2(h1, h1_bf16, w_ff1_bf16, row(b_ff1[0]), w_ff2_bf16,
                   row(b_ff2[0]), row(ln2_g[0]), row(ln2_b[0]), alpha)
    return out.reshape(batch, seq, d)
```

```python
import functools

import jax
import jax.numpy as jnp
import numpy as np
from jax import lax
from jax.experimental import pallas as pl
from jax.experimental.pallas import tpu as pltpu

F32 = jnp.float32
BF16 = jnp.bfloat16

LN_EPS = 1e-5
LOG2_E = 1.4426950408889634
EXP2_CLAMP = 100.0
POOL_WINDOWS = (2, 4, 8, 16)
POOL_HALO = 16
HEAD_DIM = 128

VMEM_LIMIT_BYTES = 56 * 1024 * 1024

INPROJ_TM = 1024
INPROJ_TN = 1024
ATTN_T = 256
ATTN_HEADS = 8
ATTN_SKEW = 2
ATTN_LAG = 4
MIX_TM = 512
FFN_TM = 512
FFN_TF = 1024
LN_ROWS = 128


def _layer_norm(x, g, b):
    mu = jnp.mean(x, axis=-1, keepdims=True)
    xc = x - mu
    var = jnp.mean(xc * xc, axis=-1, keepdims=True)
    return xc * lax.rsqrt(var + LN_EPS) * g + b


def _inproj_kernel(x_hbm, g_ref, b_ref, w_ref, pool_ref, qkv_ref, x_scr, h_scr, sem, *, q_scale):
    i = pl.program_id(0)
    j = pl.program_id(1)
    n_i = pl.num_programs(0)
    last_j = pl.num_programs(1) - 1
    tm = x_scr.shape[0]

    def x_copy(blk):
        return pltpu.make_async_copy(x_hbm.at[pl.ds(blk * tm, tm), :], x_scr, sem)

    def normalise(slot):
        for r in range(tm // LN_ROWS):
            rows = slice(r * LN_ROWS, (r + 1) * LN_ROWS)
            h_scr[slot, rows, :] = _layer_norm(
                x_scr[rows, :], g_ref[...], b_ref[...]).astype(BF16)

    def project(h):
        return jnp.dot(h, w_ref[...].astype(BF16), preferred_element_type=F32)

    @pl.when((i == 0) & (j == 0))
    def _():
        x_copy(0).start()
        x_copy(0).wait()
        normalise(0)

        @pl.when(n_i > 1)
        def _():
            x_copy(1).start()

    @pl.when(j < last_j)
    def _():
        acc = project(h_scr[i % 2])

        @pl.when(j == 0)
        def _():
            pool_ref[...] = acc

        @pl.when(j == 1)
        def _():
            qkv_ref[...] = (acc * q_scale).astype(BF16)

        @pl.when(j > 1)
        def _():
            qkv_ref[...] = acc.astype(BF16)

    def last_step_and_next_norm(slot):
        x_copy(i + 1).wait()
        acc = project(h_scr[slot])
        normalise(1 - slot)
        qkv_ref[...] = acc.astype(BF16)

        @pl.when(i + 2 < n_i)
        def _():
            x_copy(i + 2).start()

    for parity in (0, 1):
        @pl.when((j == last_j) & (i < n_i - 1) & (i % 2 == parity))
        def _():
            last_step_and_next_norm(parity)

    @pl.when((j == last_j) & (i == n_i - 1))
    def _():
        qkv_ref[...] = project(h_scr[i % 2]).astype(BF16)


def _in_proj(x2, g, b, w_in, pool_width, q_scale):
    m, d = x2.shape
    n = w_in.shape[1]
    tm, tn = INPROJ_TM, INPROJ_TN
    assert pool_width == tn and n == 4 * tn and m % tm == 0
    return pl.pallas_call(
        functools.partial(_inproj_kernel, q_scale=q_scale),
        grid=(m // tm, n // tn),
        in_specs=[
            pl.BlockSpec(memory_space=pl.ANY),
            pl.BlockSpec((1, d), lambda i, j: (0, 0)),
            pl.BlockSpec((1, d), lambda i, j: (0, 0)),
            pl.BlockSpec((d, tn), lambda i, j: (0, j)),
        ],
        out_specs=[
            pl.BlockSpec((tm, tn), lambda i, j: (i, 0)),
            pl.BlockSpec((tm, tn), lambda i, j: (i, jnp.maximum(j - 1, 0))),
        ],
        out_shape=[
            jax.ShapeDtypeStruct((m, pool_width), F32),
            jax.ShapeDtypeStruct((m, n - pool_width), BF16),
        ],
        scratch_shapes=[pltpu.VMEM((tm, d), F32), pltpu.VMEM((2, tm, d), BF16),
                        pltpu.SemaphoreType.DMA(())],
        compiler_params=pltpu.CompilerParams(
            dimension_semantics=("arbitrary", "arbitrary"),
            vmem_limit_bytes=VMEM_LIMIT_BYTES),
        name="in_proj",
    )(x2, g, b, w_in)


def _attn_kernel(*refs, n_cast):
    q_ref, k_ref, v_ref = refs[:3]
    cast_in = refs[3:3 + n_cast]
    o_ref = refs[3 + n_cast]
    cast_out = refs[4 + n_cast:4 + 2 * n_cast]
    carry_scr, acc_scr, w_scr, a_scr, cum_scr = refs[4 + 2 * n_cast:]
    cum_scr[...] = jnp.where(
        lax.broadcasted_iota(jnp.int32, cum_scr.shape, 0)
        >= lax.broadcasted_iota(jnp.int32, cum_scr.shape, 1), 1.0, 0.0).astype(BF16)
    for src, dst in zip(cast_in, cast_out):
        dst[...] = src[...].astype(dst.dtype)
    qi = pl.program_id(2)
    t = ATTN_T
    nh = q_ref.shape[1] // HEAD_DIM
    skew = ATTN_SKEW
    heads = range(nh)
    cols = [slice(h * HEAD_DIM, (h + 1) * HEAD_DIM) for h in heads]

    def key_rows(kt):
        return pl.ds(pl.multiple_of(kt * t, t), t)

    def scores(kt, h):
        return lax.dot_general(q_ref[:, cols[h]], k_ref[key_rows(kt), cols[h]],
                               (((1,), (1,)), ((), ())),
                               preferred_element_type=F32)

    def deferred_values(kt, j):
        h = nh - skew + j
        acc_scr[h] += jnp.dot(a_scr[j], v_ref[key_rows(kt), cols[h]],
                              preferred_element_type=F32)

    def sweep(kts, first):
        n_items = len(kts) * nh
        item = lambda n: (kts[n // nh], n % nh)
        kt_after = jnp.maximum(kts[-1] - 1, 0)
        if first:
            mask = (lax.broadcasted_iota(jnp.int32, (t, t), 1)
                    < lax.broadcasted_iota(jnp.int32, (t, t), 0))

        def on_diagonal(n):
            return first and n < nh

        def softplus_cumsum(n, w):
            sp = jnp.maximum(
                w, jnp.log(1.0 + jnp.exp2(jnp.minimum(w, EXP2_CLAMP))) * LOG2_E)
            if on_diagonal(n):
                sp = jnp.where(mask, sp, 0.0)
            return jnp.dot(sp.astype(BF16), cum_scr[...], preferred_element_type=F32)

        def weights_values(n, w, cs):
            kt, h = item(n)
            diag = on_diagonal(n)
            total = jnp.broadcast_to(cs[:, 0:1], (t, HEAD_DIM))
            if diag:
                a = jnp.where(mask, jnp.exp2(w - cs), 0.0)
            else:
                carry = carry_scr[h]
                a = jnp.exp2(w - cs - jnp.concatenate([carry] * (t // HEAD_DIM), axis=1))
            carry_scr[h] = total if diag else carry + total
            if n >= n_items - skew:
                a_scr[n - (n_items - skew)] = a.astype(BF16)
                if diag:
                    acc_scr[h] = jnp.zeros((t, HEAD_DIM), F32)
                return
            av = jnp.dot(a.astype(BF16), v_ref[key_rows(kt), cols[h]],
                         preferred_element_type=F32)
            if diag:
                acc_scr[h] = av
            else:
                acc_scr[h] += av

        w, cs, w_after = {}, {}, {}
        for j in range(skew):
            w[j] = scores(*item(j)) if first else w_scr[j]
        for n in range(n_items):
            if n + skew < n_items:
                w[n + skew] = scores(*item(n + skew))
            else:
                w_after[n + skew - n_items] = scores(kt_after, n + skew - n_items)
            cs[n] = softplus_cumsum(n, w[n])
            if n >= ATTN_LAG:
                weights_values(n - ATTN_LAG, w.pop(n - ATTN_LAG), cs.pop(n - ATTN_LAG))
            if n < skew and not first:
                deferred_values(kts[0] + 1, n)
        for n in range(n_items - ATTN_LAG, n_items):
            weights_values(n, w.pop(n), cs.pop(n))
        for j in range(skew):
            w_scr[j] = w_after[j]

    @pl.when(qi == 0)
    def _():
        sweep([qi], True)

    @pl.when(qi > 0)
    def _():
        sweep([qi, qi - 1], True)
        rest = qi - 1

        def pair_body(n, _):
            kt = rest - 1 - 2 * n
            sweep([kt, kt - 1], False)
            return 0

        lax.fori_loop(0, rest // 2, pair_body, 0)

        @pl.when(rest % 2 == 1)
        def _():
            sweep([qi * 0], False)

    for j in range(skew):
        deferred_values(0, j)
    for h in heads:
        o_ref[:, cols[h]] = acc_scr[h].astype(o_ref.dtype)


def _sb_attention(qkv, batch, seq, heads, weights_f32):
    m = qkv.shape[0]
    nh = ATTN_HEADS
    t = ATTN_T
    assert heads % nh == 0 and seq % t == 0
    groups = heads // nh
    nq = seq // t
    width = nh * HEAD_DIM
    steps = batch * groups * nq
    step = lambda b, g, i: ((b * groups + g) * nq + i, 0)
    slabs = []
    for wgt in weights_f32:
        rows, ncol = wgt.shape
        assert rows % (steps * 16) == 0
        slabs.append(pl.BlockSpec((rows // steps, ncol), step))
    out = pl.pallas_call(
        functools.partial(_attn_kernel, n_cast=len(weights_f32)),
        grid=(batch, groups, nq),
        in_specs=[
            pl.BlockSpec((t, width), lambda b, g, i: (b * nq + i, g)),
            pl.BlockSpec((seq, width), lambda b, g, i: (b, groups + g)),
            pl.BlockSpec((seq, width), lambda b, g, i: (b, 2 * groups + g)),
        ] + slabs,
        out_specs=[pl.BlockSpec((t, width), lambda b, g, i: (b * nq + i, g))] + slabs,
        out_shape=[jax.ShapeDtypeStruct((m, heads * HEAD_DIM), BF16)]
        + [jax.ShapeDtypeStruct(wgt.shape, BF16) for wgt in weights_f32],
        scratch_shapes=[pltpu.VMEM((nh, t, HEAD_DIM), F32), pltpu.VMEM((nh, t, HEAD_DIM), F32),
                        pltpu.VMEM((ATTN_SKEW, t, t), F32), pltpu.VMEM((ATTN_SKEW, t, t), BF16),
                        pltpu.VMEM((t, t), BF16)],
        compiler_params=pltpu.CompilerParams(
            dimension_semantics=("arbitrary", "arbitrary", "arbitrary"),
            vmem_limit_bytes=VMEM_LIMIT_BYTES),
        name="sb_attn",
    )(qkv, qkv, qkv, *weights_f32)
    return out[0], out[1:]


def _mix_kernel(x_ref, up_ref, halo_ref, ysb_ref, wout_ref, wpool_ref, pscale_ref,
                ling_ref, linb_ref, l1g_ref, l1b_ref, h1_ref, h1b_ref, *, seq, alpha):
    i = pl.program_id(0)
    tm = x_ref.shape[0]
    gw = wpool_ref.shape[1]
    blk_in_seq = i % (seq // tm)
    halo = jnp.where(blk_in_seq == 0, 0.0, halo_ref[...])
    u = up_ref[...]
    ext = jnp.concatenate([halo, u], axis=0)
    t = blk_in_seq * tm + lax.broadcasted_iota(jnp.int32, (tm, 1), 0)
    parts = []
    for g, w in enumerate(POOL_WINDOWS):
        cols = slice(g * gw, (g + 1) * gw)
        s = ext[:, cols]
        d = 1
        while d < w:
            s = s + pltpu.roll(s, d, axis=0)
            d *= 2
        s = s[POOL_HALO:, :]
        count = jnp.minimum(t + 1, w).astype(F32)
        y = s * (1.0 / count) - u[:, cols]
        yp = jnp.dot(y.astype(BF16), wpool_ref[g], preferred_element_type=F32)
        parts.append((yp * pscale_ref[g:g + 1, :]).astype(BF16))
    mix_in = jnp.concatenate(parts + [ysb_ref[...]], axis=1)
    mix = jnp.dot(mix_in, wout_ref[...], preferred_element_type=F32)
    h = _layer_norm(x_ref[...], ling_ref[...], linb_ref[...])
    h1 = _layer_norm(alpha * h + mix, l1g_ref[...], l1b_ref[...])
    h1_ref[...] = h1
    h1b_ref[...] = h1.astype(BF16)


def _mix_ln1(x2, u_pool, y_sb, w_out_bf16, w_pool_bf16, pool_scale, ln_in_g, ln_in_b,
             ln1_g, ln1_b, seq, alpha):
    m, d = x2.shape
    pw = u_pool.shape[1]
    sw = y_sb.shape[1]
    tm = MIX_TM
    assert seq % tm == 0 and tm % POOL_HALO == 0 and max(POOL_WINDOWS) <= POOL_HALO
    halo_blocks = tm // POOL_HALO
    const = lambda i: (0, 0)
    return pl.pallas_call(
        functools.partial(_mix_kernel, seq=seq, alpha=alpha),
        grid=(m // tm,),
        in_specs=[
            pl.BlockSpec((tm, d), lambda i: (i, 0)),
            pl.BlockSpec((tm, pw), lambda i: (i, 0)),
            pl.BlockSpec((POOL_HALO, pw), lambda i: (jnp.maximum(i * halo_blocks - 1, 0), 0)),
            pl.BlockSpec((tm, sw), lambda i: (i, 0)),
            pl.BlockSpec(w_out_bf16.shape, const, pipeline_mode=pl.Buffered(1)),
            pl.BlockSpec(w_pool_bf16.shape, lambda i: (0, 0, 0)),
            pl.BlockSpec(pool_scale.shape, const),
            pl.BlockSpec((1, d), const),
            pl.BlockSpec((1, d), const),
            pl.BlockSpec((1, d), const),
            pl.BlockSpec((1, d), const),
        ],
        out_specs=[pl.BlockSpec((tm, d), lambda i: (i, 0))] * 2,
        out_shape=[jax.ShapeDtypeStruct((m, d), F32), jax.ShapeDtypeStruct((m, d), BF16)],
        compiler_params=pltpu.CompilerParams(
            dimension_semantics=("arbitrary",),
            vmem_limit_bytes=VMEM_LIMIT_BYTES),
        name="mix_ln1",
    )(x2, u_pool, u_pool, y_sb, w_out_bf16, w_pool_bf16, pool_scale,
      ln_in_g, ln_in_b, ln1_g, ln1_b)


def _ffn_kernel(h1_ref, h1b_ref, w1_ref, b1_ref, w2_ref, b2_ref, g_ref, b_ref, o_ref,
                acc_scr, *, alpha):
    i = pl.program_id(0)
    f = pl.program_id(1)
    last_i = pl.num_programs(0) - 1
    last_f = pl.num_programs(1) - 1

    def mlp_chunk():
        t = jnp.dot(h1b_ref[...], w1_ref[...], preferred_element_type=F32) + b1_ref[...]
        t = jnp.maximum(t, 0.0)
        return jnp.dot((t * t).astype(BF16), w2_ref[...], preferred_element_type=F32)

    def norm():
        return _layer_norm(acc_scr[...], g_ref[...], b_ref[...])

    @pl.when((f == 0) & (i == 0))
    def _():
        acc_scr[...] = mlp_chunk()

    @pl.when((f == 0) & (i > 0))
    def _():
        o_ref[...] = norm()
        acc_scr[...] = mlp_chunk()

    @pl.when((f > 0) & (f < last_f))
    def _():
        acc_scr[...] += mlp_chunk()

    @pl.when(f == last_f)
    def _():
        acc_scr[...] = alpha * h1_ref[...] + (acc_scr[...] + mlp_chunk() + b2_ref[...])

    @pl.when((f == last_f) & (i == last_i))
    def _():
        o_ref[...] = norm()


def _ffn_ln2(h1, h1_bf16, w1_bf16, b1, w2_bf16, b2, g, b, alpha):
    m, d = h1.shape
    dff = w1_bf16.shape[1]
    tm, tf = FFN_TM, FFN_TF
    assert m % tm == 0 and dff % tf == 0 and dff // tf >= 2
    n_i, n_f = m // tm, dff // tf
    const = lambda i, f: (0, 0)

    def out_block(i, f):
        final = (i == n_i - 1) & (f == n_f - 1)
        return (jnp.where(final, i, jnp.maximum(i - 1, 0)), 0)

    return pl.pallas_call(
        functools.partial(_ffn_kernel, alpha=alpha),
        grid=(n_i, n_f),
        in_specs=[
            pl.BlockSpec((tm, d), lambda i, f: (i, 0)),
            pl.BlockSpec((tm, d), lambda i, f: (i, 0)),
            pl.BlockSpec((d, tf), lambda i, f: (0, f)),
            pl.BlockSpec((1, tf), lambda i, f: (0, f)),
            pl.BlockSpec((tf, d), lambda i, f: (f, 0)),
            pl.BlockSpec((1, d), const),
            pl.BlockSpec((1, d), const),
            pl.BlockSpec((1, d), const),
        ],
        out_specs=pl.BlockSpec((tm, d), out_block),
        out_shape=jax.ShapeDtypeStruct((m, d), F32),
        scratch_shapes=[pltpu.VMEM((tm, d), F32)],
        compiler_params=pltpu.CompilerParams(
            dimension_semantics=("arbitrary", "arbitrary"),
            vmem_limit_bytes=VMEM_LIMIT_BYTES),
        name="ffn_ln2",
    )(h1, h1_bf16, w1_bf16, b1, w2_bf16, b2, g, b)


def kernel(x, ln_in_g, ln_in_b, w_in, w_pool, pool_scale, w_out, ln1_g, ln1_b,
           w_ff1, b_ff1, w_ff2, b_ff2, ln2_g, ln2_b):
    batch, seq, d = x.shape
    depth, groups, gw, _ = w_pool.shape
    assert depth == 1, "single trunk layer"
    pool_width = groups * gw
    sb_width = (w_in.shape[2] - pool_width) // 3
    heads = sb_width // HEAD_DIM
    alpha = float((2.0 * depth) ** 0.25)
    m = batch * seq

    x2 = x.reshape(m, d)
    row = lambda p: p.reshape(1, -1)

    q_scale = float(LOG2_E / np.sqrt(np.float32(HEAD_DIM)))
    u_pool, qkv = _in_proj(x2, row(ln_in_g), row(ln_in_b), w_in[0], pool_width, q_scale)
    y_sb, (w_out_bf16, w_ff1_bf16, w_ff2_bf16, w_pool_bf16) = _sb_attention(
        qkv, batch, seq, heads,
        [w_out[0], w_ff1[0], w_ff2[0], w_pool[0].reshape(groups * gw, gw)])
    h1, h1_bf16 = _mix_ln1(x2, u_pool, y_sb, w_out_bf16, w_pool_bf16.reshape(groups, gw, gw),
                  pool_scale[0], row(ln_in_g), row(ln_in_b), row(ln1_g[0]), row(ln1_b[0]),
                  seq, alpha)
    out = _ffn_ln2(h1, h1_bf16, w_ff1_bf16, row(b_ff1[0]), w_ff2_bf16,
                   row(b_ff2[0]), row(ln2_g[0]), row(ln2_b[0]), alpha)
    return out.reshape(batch, seq, d)
```

```python
import functools

import jax
import jax.numpy as jnp
import numpy as np
from jax import lax
from jax.experimental import pallas as pl
from jax.experimental.pallas import tpu as pltpu

F32 = jnp.float32
BF16 = jnp.bfloat16

LN_EPS = 1e-5
LOG2_E = 1.4426950408889634
EXP2_CLAMP = 100.0
POOL_WINDOWS = (2, 4, 8, 16)
POOL_HALO = 16
HEAD_DIM = 128

VMEM_LIMIT_BYTES = 56 * 1024 * 1024

INPROJ_TM = 1024
INPROJ_TN = 1024
ATTN_T = 256
ATTN_HEADS = 8
ATTN_SKEW = 2
ATTN_LAG = 4
ATTN_CS_GROUP = 2
MIX_TM = 512
FFN_TM = 512
FFN_TF = 1024
LN_ROWS = 128


def _layer_norm(x, g, b):
    mu = jnp.mean(x, axis=-1, keepdims=True)
    xc = x - mu
    var = jnp.mean(xc * xc, axis=-1, keepdims=True)
    return xc * lax.rsqrt(var + LN_EPS) * g + b


def _inproj_kernel(x_hbm, g_ref, b_ref, w_ref, pool_ref, qkv_ref, x_scr, h_scr, sem, *, q_scale):
    i = pl.program_id(0)
    j = pl.program_id(1)
    n_i = pl.num_programs(0)
    last_j = pl.num_programs(1) - 1
    tm = x_scr.shape[0]

    def x_copy(blk):
        return pltpu.make_async_copy(x_hbm.at[pl.ds(blk * tm, tm), :], x_scr, sem)

    def normalise(slot):
        for r in range(tm // LN_ROWS):
            rows = slice(r * LN_ROWS, (r + 1) * LN_ROWS)
            h_scr[slot, rows, :] = _layer_norm(
                x_scr[rows, :], g_ref[...], b_ref[...]).astype(BF16)

    def project(h):
        return jnp.dot(h, w_ref[...].astype(BF16), preferred_element_type=F32)

    @pl.when((i == 0) & (j == 0))
    def _():
        x_copy(0).start()
        x_copy(0).wait()
        normalise(0)

        @pl.when(n_i > 1)
        def _():
            x_copy(1).start()

    @pl.when(j < last_j)
    def _():
        acc = project(h_scr[i % 2])

        @pl.when(j == 0)
        def _():
            pool_ref[...] = acc

        @pl.when(j == 1)
        def _():
            qkv_ref[...] = (acc * q_scale).astype(BF16)

        @pl.when(j > 1)
        def _():
            qkv_ref[...] = acc.astype(BF16)

    def last_step_and_next_norm(slot):
        x_copy(i + 1).wait()
        acc = project(h_scr[slot])
        normalise(1 - slot)
        qkv_ref[...] = acc.astype(BF16)

        @pl.when(i + 2 < n_i)
        def _():
            x_copy(i + 2).start()

    for parity in (0, 1):
        @pl.when((j == last_j) & (i < n_i - 1) & (i % 2 == parity))
        def _():
            last_step_and_next_norm(parity)

    @pl.when((j == last_j) & (i == n_i - 1))
    def _():
        qkv_ref[...] = project(h_scr[i % 2]).astype(BF16)


def _in_proj(x2, g, b, w_in, pool_width, q_scale):
    m, d = x2.shape
    n = w_in.shape[1]
    tm, tn = INPROJ_TM, INPROJ_TN
    assert pool_width == tn and n == 4 * tn and m % tm == 0
    return pl.pallas_call(
        functools.partial(_inproj_kernel, q_scale=q_scale),
        grid=(m // tm, n // tn),
        in_specs=[
            pl.BlockSpec(memory_space=pl.ANY),
            pl.BlockSpec((1, d), lambda i, j: (0, 0)),
            pl.BlockSpec((1, d), lambda i, j: (0, 0)),
            pl.BlockSpec((d, tn), lambda i, j: (0, j)),
        ],
        out_specs=[
            pl.BlockSpec((tm, tn), lambda i, j: (i, 0)),
            pl.BlockSpec((tm, tn), lambda i, j: (i, jnp.maximum(j - 1, 0))),
        ],
        out_shape=[
            jax.ShapeDtypeStruct((m, pool_width), F32),
            jax.ShapeDtypeStruct((m, n - pool_width), BF16),
        ],
        scratch_shapes=[pltpu.VMEM((tm, d), F32), pltpu.VMEM((2, tm, d), BF16),
                        pltpu.SemaphoreType.DMA(())],
        compiler_params=pltpu.CompilerParams(
            dimension_semantics=("arbitrary", "arbitrary"),
            vmem_limit_bytes=VMEM_LIMIT_BYTES),
        name="in_proj",
    )(x2, g, b, w_in)


def _attn_kernel(*refs, n_cast):
    q_ref, k_ref, v_ref = refs[:3]
    cast_in = refs[3:3 + n_cast]
    o_ref = refs[3 + n_cast]
    cast_out = refs[4 + n_cast:4 + 2 * n_cast]
    carry_scr, acc_scr, w_scr, a_scr, cum_scr = refs[4 + 2 * n_cast:]
    cum_scr[...] = jnp.where(
        lax.broadcasted_iota(jnp.int32, cum_scr.shape, 0)
        >= lax.broadcasted_iota(jnp.int32, cum_scr.shape, 1), 1.0, 0.0).astype(BF16)
    for src, dst in zip(cast_in, cast_out):
        dst[...] = src[...].astype(dst.dtype)
    qi = pl.program_id(2)
    t = ATTN_T
    nh = q_ref.shape[1] // HEAD_DIM
    skew = ATTN_SKEW
    heads = range(nh)
    cols = [slice(h * HEAD_DIM, (h + 1) * HEAD_DIM) for h in heads]

    def key_rows(kt):
        return pl.ds(pl.multiple_of(kt * t, t), t)

    def scores(kt, h):
        return lax.dot_general(q_ref[:, cols[h]], k_ref[key_rows(kt), cols[h]],
                               (((1,), (1,)), ((), ())),
                               preferred_element_type=F32)

    def deferred_values(kt, j):
        h = nh - skew + j
        acc_scr[h] += jnp.dot(a_scr[j], v_ref[key_rows(kt), cols[h]],
                              preferred_element_type=F32)

    def sweep(kts, first):
        n_items = len(kts) * nh
        item = lambda n: (kts[n // nh], n % nh)
        kt_after = jnp.maximum(kts[-1] - 1, 0)
        if first:
            mask = (lax.broadcasted_iota(jnp.int32, (t, t), 1)
                    < lax.broadcasted_iota(jnp.int32, (t, t), 0))

        def on_diagonal(n):
            return first and n < nh

        def softplus_tile(n, w):
            sp = jnp.maximum(
                w, jnp.log(1.0 + jnp.exp2(jnp.minimum(w, EXP2_CLAMP))) * LOG2_E)
            if on_diagonal(n):
                sp = jnp.where(mask, sp, 0.0)
            return sp.astype(BF16)

        def cumsum_group(sps):
            both = jnp.dot(jnp.concatenate(sps, axis=0), cum_scr[...],
                           preferred_element_type=F32)
            return [both[c * t:(c + 1) * t] for c in range(len(sps))]

        def weights_values(n, w, cs):
            kt, h = item(n)
            diag = on_diagonal(n)
            total = jnp.broadcast_to(cs[:, 0:1], (t, HEAD_DIM))
            if diag:
                a = jnp.where(mask, jnp.exp2(w - cs), 0.0)
            else:
                carry = carry_scr[h]
                a = jnp.exp2(w - cs - jnp.concatenate([carry] * (t // HEAD_DIM), axis=1))
            carry_scr[h] = total if diag else carry + total
            if n >= n_items - skew:
                a_scr[n - (n_items - skew)] = a.astype(BF16)
                if diag:
                    acc_scr[h] = jnp.zeros((t, HEAD_DIM), F32)
                return
            av = jnp.dot(a.astype(BF16), v_ref[key_rows(kt), cols[h]],
                         preferred_element_type=F32)
            if diag:
                acc_scr[h] = av
            else:
                acc_scr[h] += av

        w, sp, cs, w_after = {}, {}, {}, {}
        for j in range(skew):
            w[j] = scores(*item(j)) if first else w_scr[j]
        for n in range(n_items):
            if n + skew < n_items:
                w[n + skew] = scores(*item(n + skew))
            else:
                w_after[n + skew - n_items] = scores(kt_after, n + skew - n_items)
            sp[n] = softplus_tile(n, w[n])
            if n % ATTN_CS_GROUP == ATTN_CS_GROUP - 1:
                group = range(n - ATTN_CS_GROUP + 1, n + 1)
                for m, c in zip(group, cumsum_group([sp.pop(m) for m in group])):
                    cs[m] = c
            if n >= ATTN_LAG:
                weights_values(n - ATTN_LAG, w.pop(n - ATTN_LAG), cs.pop(n - ATTN_LAG))
            if n < skew and not first:
                deferred_values(kts[0] + 1, n)
        for n in range(n_items - ATTN_LAG, n_items):
            weights_values(n, w.pop(n), cs.pop(n))
        for j in range(skew):
            w_scr[j] = w_after[j]

    @pl.when(qi == 0)
    def _():
        sweep([qi], True)

    @pl.when(qi > 0)
    def _():
        sweep([qi, qi - 1], True)
        rest = qi - 1

        def pair_body(n, _):
            kt = rest - 1 - 2 * n
            sweep([kt, kt - 1], False)
            return 0

        lax.fori_loop(0, rest // 2, pair_body, 0)

        @pl.when(rest % 2 == 1)
        def _():
            sweep([qi * 0], False)

    for j in range(skew):
        deferred_values(0, j)
    for h in heads:
        o_ref[:, cols[h]] = acc_scr[h].astype(o_ref.dtype)


def _sb_attention(qkv, batch, seq, heads, weights_f32):
    m = qkv.shape[0]
    nh = ATTN_HEADS
    t = ATTN_T
    assert heads % nh == 0 and seq % t == 0
    groups = heads // nh
    nq = seq // t
    width = nh * HEAD_DIM
    steps = batch * groups * nq
    step = lambda b, g, i: ((b * groups + g) * nq + i, 0)
    slabs = []
    for wgt in weights_f32:
        rows, ncol = wgt.shape
        assert rows % (steps * 16) == 0
        slabs.append(pl.BlockSpec((rows // steps, ncol), step))
    out = pl.pallas_call(
        functools.partial(_attn_kernel, n_cast=len(weights_f32)),
        grid=(batch, groups, nq),
        in_specs=[
            pl.BlockSpec((t, width), lambda b, g, i: (b * nq + i, g)),
            pl.BlockSpec((seq, width), lambda b, g, i: (b, groups + g)),
            pl.BlockSpec((seq, width), lambda b, g, i: (b, 2 * groups + g)),
        ] + slabs,
        out_specs=[pl.BlockSpec((t, width), lambda b, g, i: (b * nq + i, g))] + slabs,
        out_shape=[jax.ShapeDtypeStruct((m, heads * HEAD_DIM), BF16)]
        + [jax.ShapeDtypeStruct(wgt.shape, BF16) for wgt in weights_f32],
        scratch_shapes=[pltpu.VMEM((nh, t, HEAD_DIM), F32), pltpu.VMEM((nh, t, HEAD_DIM), F32),
                        pltpu.VMEM((ATTN_SKEW, t, t), F32), pltpu.VMEM((ATTN_SKEW, t, t), BF16),
                        pltpu.VMEM((t, t), BF16)],
        compiler_params=pltpu.CompilerParams(
            dimension_semantics=("arbitrary", "arbitrary", "arbitrary"),
            vmem_limit_bytes=VMEM_LIMIT_BYTES),
        name="sb_attn",
    )(qkv, qkv, qkv, *weights_f32)
    return out[0], out[1:]


def _mix_kernel(x_ref, up_ref, halo_ref, ysb_ref, wout_ref, wpool_ref, pscale_ref,
                ling_ref, linb_ref, l1g_ref, l1b_ref, h1_ref, h1b_ref, *, seq, alpha):
    i = pl.program_id(0)
    tm = x_ref.shape[0]
    gw = wpool_ref.shape[1]
    blk_in_seq = i % (seq // tm)
    halo = jnp.where(blk_in_seq == 0, 0.0, halo_ref[...])
    u = up_ref[...]
    ext = jnp.concatenate([halo, u], axis=0)
    t = blk_in_seq * tm + lax.broadcasted_iota(jnp.int32, (tm, 1), 0)
    parts = []
    for g, w in enumerate(POOL_WINDOWS):
        cols = slice(g * gw, (g + 1) * gw)
        s = ext[:, cols]
        d = 1
        while d < w:
            s = s + pltpu.roll(s, d, axis=0)
            d *= 2
        s = s[POOL_HALO:, :]
        count = jnp.minimum(t + 1, w).astype(F32)
        y = s * (1.0 / count) - u[:, cols]
        yp = jnp.dot(y.astype(BF16), wpool_ref[g], preferred_element_type=F32)
        parts.append((yp * pscale_ref[g:g + 1, :]).astype(BF16))
    mix_in = jnp.concatenate(parts + [ysb_ref[...]], axis=1)
    mix = jnp.dot(mix_in, wout_ref[...], preferred_element_type=F32)
    h = _layer_norm(x_ref[...], ling_ref[...], linb_ref[...])
    h1 = _layer_norm(alpha * h + mix, l1g_ref[...], l1b_ref[...])
    h1_ref[...] = h1
    h1b_ref[...] = h1.astype(BF16)


def _mix_ln1(x2, u_pool, y_sb, w_out_bf16, w_pool_bf16, pool_scale, ln_in_g, ln_in_b,
             ln1_g, ln1_b, seq, alpha):
    m, d = x2.shape
    pw = u_pool.shape[1]
    sw = y_sb.shape[1]
    tm = MIX_TM
    assert seq % tm == 0 and tm % POOL_HALO == 0 and max(POOL_WINDOWS) <= POOL_HALO
    halo_blocks = tm // POOL_HALO
    const = lambda i: (0, 0)
    return pl.pallas_call(
        functools.partial(_mix_kernel, seq=seq, alpha=alpha),
        grid=(m // tm,),
        in_specs=[
            pl.BlockSpec((tm, d), lambda i: (i, 0)),
            pl.BlockSpec((tm, pw), lambda i: (i, 0)),
            pl.BlockSpec((POOL_HALO, pw), lambda i: (jnp.maximum(i * halo_blocks - 1, 0), 0)),
            pl.BlockSpec((tm, sw), lambda i: (i, 0)),
            pl.BlockSpec(w_out_bf16.shape, const, pipeline_mode=pl.Buffered(1)),
            pl.BlockSpec(w_pool_bf16.shape, lambda i: (0, 0, 0)),
            pl.BlockSpec(pool_scale.shape, const),
            pl.BlockSpec((1, d), const),
            pl.BlockSpec((1, d), const),
            pl.BlockSpec((1, d), const),
            pl.BlockSpec((1, d), const),
        ],
        out_specs=[pl.BlockSpec((tm, d), lambda i: (i, 0))] * 2,
        out_shape=[jax.ShapeDtypeStruct((m, d), F32), jax.ShapeDtypeStruct((m, d), BF16)],
        compiler_params=pltpu.CompilerParams(
            dimension_semantics=("arbitrary",),
            vmem_limit_bytes=VMEM_LIMIT_BYTES),
        name="mix_ln1",
    )(x2, u_pool, u_pool, y_sb, w_out_bf16, w_pool_bf16, pool_scale,
      ln_in_g, ln_in_b, ln1_g, ln1_b)


def _ffn_kernel(h1_ref, h1b_ref, w1_ref, b1_ref, w2_ref, b2_ref, g_ref, b_ref, o_ref,
                acc_scr, *, alpha):
    i = pl.program_id(0)
    f = pl.program_id(1)
    last_i = pl.num_programs(0) - 1
    last_f = pl.num_programs(1) - 1

    def mlp_chunk():
        t = jnp.dot(h1b_ref[...], w1_ref[...], preferred_element_type=F32) + b1_ref[...]
        t = jnp.maximum(t, 0.0)
        return jnp.dot((t * t).astype(BF16), w2_ref[...], preferred_element_type=F32)

    def norm():
        return _layer_norm(acc_scr[...], g_ref[...], b_ref[...])

    @pl.when((f == 0) & (i == 0))
    def _():
        acc_scr[...] = mlp_chunk()

    @pl.when((f == 0) & (i > 0))
    def _():
        o_ref[...] = norm()
        acc_scr[...] = mlp_chunk()

    @pl.when((f > 0) & (f < last_f))
    def _():
        acc_scr[...] += mlp_chunk()

    @pl.when(f == last_f)
    def _():
        acc_scr[...] = alpha * h1_ref[...] + (acc_scr[...] + mlp_chunk() + b2_ref[...])

    @pl.when((f == last_f) & (i == last_i))
    def _():
        o_ref[...] = norm()


def _ffn_ln2(h1, h1_bf16, w1_bf16, b1, w2_bf16, b2, g, b, alpha):
    m, d = h1.shape
    dff = w1_bf16.shape[1]
    tm, tf = FFN_TM, FFN_TF
    assert m % tm == 0 and dff % tf == 0 and dff // tf >= 2
    n_i, n_f = m // tm, dff // tf
    const = lambda i, f: (0, 0)

    def out_block(i, f):
        final = (i == n_i - 1) & (f == n_f - 1)
        return (jnp.where(final, i, jnp.maximum(i - 1, 0)), 0)

    return pl.pallas_call(
        functools.partial(_ffn_kernel, alpha=alpha),
        grid=(n_i, n_f),
        in_specs=[
            pl.BlockSpec((tm, d), lambda i, f: (i, 0)),
            pl.BlockSpec((tm, d), lambda i, f: (i, 0)),
            pl.BlockSpec((d, tf), lambda i, f: (0, f)),
            pl.BlockSpec((1, tf), lambda i, f: (0, f)),
            pl.BlockSpec((tf, d), lambda i, f: (f, 0)),
            pl.BlockSpec((1, d), const),
            pl.BlockSpec((1, d), const),
            pl.BlockSpec((1, d), const),
        ],
        out_specs=pl.BlockSpec((tm, d), out_block),
        out_shape=jax.ShapeDtypeStruct((m, d), F32),
        scratch_shapes=[pltpu.VMEM((tm, d), F32)],
        compiler_params=pltpu.CompilerParams(
            dimension_semantics=("arbitrary", "arbitrary"),
            vmem_limit_bytes=VMEM_LIMIT_BYTES),
        name="ffn_ln2",
    )(h1, h1_bf16, w1_bf16, b1, w2_bf16, b2, g, b)


def kernel(x, ln_in_g, ln_in_b, w_in, w_pool, pool_scale, w_out, ln1_g, ln1_b,
           w_ff1, b_ff1, w_ff2, b_ff2, ln2_g, ln2_b):
    batch, seq, d = x.shape
    depth, groups, gw, _ = w_pool.shape
    assert depth == 1, "single trunk layer"
    pool_width = groups * gw
    sb_width = (w_in.shape[2] - pool_width) // 3
    heads = sb_width // HEAD_DIM
    alpha = float((2.0 * depth) ** 0.25)
    m = batch * seq

    x2 = x.reshape(m, d)
    row = lambda p: p.reshape(1, -1)

    q_scale = float(LOG2_E / np.sqrt(np.float32(HEAD_DIM)))
    u_pool, qkv = _in_proj(x2, row(ln_in_g), row(ln_in_b), w_in[0], pool_width, q_scale)
    y_sb, (w_out_bf16, w_ff1_bf16, w_ff2_bf16, w_pool_bf16) = _sb_attention(
        qkv, batch, seq, heads,
        [w_out[0], w_ff1[0], w_ff2[0], w_pool[0].reshape(groups * gw, gw)])
    h1, h1_bf16 = _mix_ln1(x2, u_pool, y_sb, w_out_bf16, w_pool_bf16.reshape(groups, gw, gw),
                  pool_scale[0], row(ln_in_g), row(ln_in_b), row(ln1_g[0]), row(ln1_b[0]),
                  seq, alpha)
    out = _ffn_ln2(h1, h1_bf16, w_ff1_bf16, row(b_ff1[0]), w_ff2_bf16,
                   row(b_ff2[0]), row(ln2_g[0]), row(ln2_b[0]), alpha)
    return out.reshape(batch, seq, d)
```

```python
import functools

import jax
import jax.numpy as jnp
import numpy as np
from jax import lax
from jax.experimental import pallas as pl
from jax.experimental.pallas import tpu as pltpu

F32 = jnp.float32
BF16 = jnp.bfloat16

LN_EPS = 1e-5
LOG2_E = 1.4426950408889634
EXP2_CLAMP = 100.0
POOL_WINDOWS = (2, 4, 8, 16)
POOL_HALO = 16
HEAD_DIM = 128

VMEM_LIMIT_BYTES = 56 * 1024 * 1024

INPROJ_TM = 1024
INPROJ_TN = 1024
ATTN_TQ = 512
ATTN_TK = 256
ATTN_HEADS = 4
ATTN_SKEW = 2
ATTN_LAG = 4
MIX_TM = 512
FFN_TM = 512
FFN_TF = 1024
LN_ROWS = 128


def _layer_norm(x, g, b):
    mu = jnp.mean(x, axis=-1, keepdims=True)
    xc = x - mu
    var = jnp.mean(xc * xc, axis=-1, keepdims=True)
    return xc * lax.rsqrt(var + LN_EPS) * g + b


def _inproj_kernel(x_hbm, g_ref, b_ref, w_ref, pool_ref, qkv_ref, x_scr, h_scr, sem, *, q_scale):
    i = pl.program_id(0)
    j = pl.program_id(1)
    n_i = pl.num_programs(0)
    last_j = pl.num_programs(1) - 1
    tm = x_scr.shape[0]

    def x_copy(blk):
        return pltpu.make_async_copy(x_hbm.at[pl.ds(blk * tm, tm), :], x_scr, sem)

    def normalise(slot):
        for r in range(tm // LN_ROWS):
            rows = slice(r * LN_ROWS, (r + 1) * LN_ROWS)
            h_scr[slot, rows, :] = _layer_norm(
                x_scr[rows, :], g_ref[...], b_ref[...]).astype(BF16)

    def project(h):
        return jnp.dot(h, w_ref[...].astype(BF16), preferred_element_type=F32)

    @pl.when((i == 0) & (j == 0))
    def _():
        x_copy(0).start()
        x_copy(0).wait()
        normalise(0)

        @pl.when(n_i > 1)
        def _():
            x_copy(1).start()

    @pl.when(j < last_j)
    def _():
        acc = project(h_scr[i % 2])

        @pl.when(j == 0)
        def _():
            pool_ref[...] = acc

        @pl.when(j == 1)
        def _():
            qkv_ref[...] = (acc * q_scale).astype(BF16)

        @pl.when(j > 1)
        def _():
            qkv_ref[...] = acc.astype(BF16)

    def last_step_and_next_norm(slot):
        x_copy(i + 1).wait()
        acc = project(h_scr[slot])
        normalise(1 - slot)
        qkv_ref[...] = acc.astype(BF16)

        @pl.when(i + 2 < n_i)
        def _():
            x_copy(i + 2).start()

    for parity in (0, 1):
        @pl.when((j == last_j) & (i < n_i - 1) & (i % 2 == parity))
        def _():
            last_step_and_next_norm(parity)

    @pl.when((j == last_j) & (i == n_i - 1))
    def _():
        qkv_ref[...] = project(h_scr[i % 2]).astype(BF16)


def _in_proj(x2, g, b, w_in, pool_width, q_scale):
    m, d = x2.shape
    n = w_in.shape[1]
    tm, tn = INPROJ_TM, INPROJ_TN
    assert pool_width == tn and n == 4 * tn and m % tm == 0
    return pl.pallas_call(
        functools.partial(_inproj_kernel, q_scale=q_scale),
        grid=(m // tm, n // tn),
        in_specs=[
            pl.BlockSpec(memory_space=pl.ANY),
            pl.BlockSpec((1, d), lambda i, j: (0, 0)),
            pl.BlockSpec((1, d), lambda i, j: (0, 0)),
            pl.BlockSpec((d, tn), lambda i, j: (0, j)),
        ],
        out_specs=[
            pl.BlockSpec((tm, tn), lambda i, j: (i, 0)),
            pl.BlockSpec((tm, tn), lambda i, j: (i, jnp.maximum(j - 1, 0))),
        ],
        out_shape=[
            jax.ShapeDtypeStruct((m, pool_width), F32),
            jax.ShapeDtypeStruct((m, n - pool_width), BF16),
        ],
        scratch_shapes=[pltpu.VMEM((tm, d), F32), pltpu.VMEM((2, tm, d), BF16),
                        pltpu.SemaphoreType.DMA(())],
        compiler_params=pltpu.CompilerParams(
            dimension_semantics=("arbitrary", "arbitrary"),
            vmem_limit_bytes=VMEM_LIMIT_BYTES),
        name="in_proj",
    )(x2, g, b, w_in)


def _attn_kernel(*refs, n_cast):
    q_ref, k_ref, v_ref = refs[:3]
    cast_in = refs[3:3 + n_cast]
    o_ref = refs[3 + n_cast]
    cast_out = refs[4 + n_cast:4 + 2 * n_cast]
    carry_scr, acc_scr, w_scr, a_scr, cum_scr = refs[4 + 2 * n_cast:]
    cum_scr[...] = jnp.where(
        lax.broadcasted_iota(jnp.int32, cum_scr.shape, 0)
        >= lax.broadcasted_iota(jnp.int32, cum_scr.shape, 1), 1.0, 0.0).astype(BF16)
    for src, dst in zip(cast_in, cast_out):
        dst[...] = src[...].astype(dst.dtype)
    qi = pl.program_id(2)
    tq, tk = ATTN_TQ, ATTN_TK
    nh = q_ref.shape[1] // HEAD_DIM
    skew = ATTN_SKEW
    heads = range(nh)
    cols = [slice(h * HEAD_DIM, (h + 1) * HEAD_DIM) for h in heads]

    def key_rows(kt):
        return pl.ds(pl.multiple_of(kt * tk, tk), tk)

    def scores(kt, h):
        return lax.dot_general(q_ref[:, cols[h]], k_ref[key_rows(kt), cols[h]],
                               (((1,), (1,)), ((), ())),
                               preferred_element_type=F32)

    def deferred_values(kt, j):
        h = nh - skew + j
        acc_scr[h] += jnp.dot(a_scr[j], v_ref[key_rows(kt), cols[h]],
                              preferred_element_type=F32)

    def sweep(kts, key_offsets=None):
        first = key_offsets is not None
        n_items = len(kts) * nh
        item = lambda n: (kts[n // nh], n % nh)
        kt_after = jnp.maximum(kts[-1] - 1, 0)
        if first:
            below = (lax.broadcasted_iota(jnp.int32, (tk, tk), 1)
                     < lax.broadcasted_iota(jnp.int32, (tk, tk), 0))

        def masked(n, x):
            off = key_offsets[n // nh]
            parts = []
            for r in range(tq // tk):
                rows = x[r * tk:(r + 1) * tk]
                if r * tk < off:
                    parts.append(jnp.zeros_like(rows))
                elif r * tk == off:
                    parts.append(jnp.where(below, rows, 0.0))
                else:
                    parts.append(rows)
            return jnp.concatenate(parts, axis=0)

        def softplus_tile(n, w):
            sp = jnp.maximum(
                w, jnp.log(1.0 + jnp.exp2(jnp.minimum(w, EXP2_CLAMP))) * LOG2_E)
            if first:
                sp = masked(n, sp)
            return sp.astype(BF16)

        def weights_values(n, w, cs):
            kt, h = item(n)
            start = first and n < nh
            total = jnp.broadcast_to(cs[:, 0:1], (tq, HEAD_DIM))
            if start:
                a = jnp.exp2(w - cs)
            else:
                carry = carry_scr[h]
                a = jnp.exp2(w - cs - jnp.concatenate([carry] * (tk // HEAD_DIM), axis=1))
            if first:
                a = masked(n, a)
            carry_scr[h] = total if start else carry + total
            if n >= n_items - skew:
                a_scr[n - (n_items - skew)] = a.astype(BF16)
                if start:
                    acc_scr[h] = jnp.zeros((tq, HEAD_DIM), F32)
                return
            av = jnp.dot(a.astype(BF16), v_ref[key_rows(kt), cols[h]],
                         preferred_element_type=F32)
            if start:
                acc_scr[h] = av
            else:
                acc_scr[h] += av

        w, cs, w_after = {}, {}, {}
        for j in range(skew):
            w[j] = scores(*item(j)) if first else w_scr[j]
        for n in range(n_items):
            if n + skew < n_items:
                w[n + skew] = scores(*item(n + skew))
            else:
                w_after[n + skew - n_items] = scores(kt_after, n + skew - n_items)
            cs[n] = jnp.dot(softplus_tile(n, w[n]), cum_scr[...], preferred_element_type=F32)
            if n >= ATTN_LAG:
                weights_values(n - ATTN_LAG, w.pop(n - ATTN_LAG), cs.pop(n - ATTN_LAG))
            if n < skew and not first:
                deferred_values(kts[0] + 1, n)
        for n in range(n_items - ATTN_LAG, n_items):
            weights_values(n, w.pop(n), cs.pop(n))
        for j in range(skew):
            w_scr[j] = w_after[j]

    top = (qi + 1) * (tq // tk) - 1
    sweep([top, top - 1], [tk, 0])

    def pair_body(n, _):
        kt = top - 2 - 2 * n
        sweep([kt, kt - 1])
        return 0

    lax.fori_loop(0, qi, pair_body, 0)
    for j in range(skew):
        deferred_values(0, j)
    for h in heads:
        o_ref[:, cols[h]] = acc_scr[h].astype(o_ref.dtype)


def _sb_attention(qkv, batch, seq, heads, weights_f32):
    m = qkv.shape[0]
    nh = ATTN_HEADS
    tq, tk = ATTN_TQ, ATTN_TK
    assert heads % nh == 0 and seq % tq == 0 and tq == 2 * tk
    groups = heads // nh
    nq = seq // tq
    width = nh * HEAD_DIM
    steps = batch * groups * nq
    step = lambda b, g, i: ((b * groups + g) * nq + i, 0)
    slabs = []
    for wgt in weights_f32:
        rows, ncol = wgt.shape
        assert rows % (steps * 16) == 0
        slabs.append(pl.BlockSpec((rows // steps, ncol), step))
    out = pl.pallas_call(
        functools.partial(_attn_kernel, n_cast=len(weights_f32)),
        grid=(batch, groups, nq),
        in_specs=[
            pl.BlockSpec((tq, width), lambda b, g, i: (b * nq + i, g)),
            pl.BlockSpec((seq, width), lambda b, g, i: (b, groups + g)),
            pl.BlockSpec((seq, width), lambda b, g, i: (b, 2 * groups + g)),
        ] + slabs,
        out_specs=[pl.BlockSpec((tq, width), lambda b, g, i: (b * nq + i, g))] + slabs,
        out_shape=[jax.ShapeDtypeStruct((m, heads * HEAD_DIM), BF16)]
        + [jax.ShapeDtypeStruct(wgt.shape, BF16) for wgt in weights_f32],
        scratch_shapes=[pltpu.VMEM((nh, tq, HEAD_DIM), F32), pltpu.VMEM((nh, tq, HEAD_DIM), F32),
                        pltpu.VMEM((ATTN_SKEW, tq, tk), F32), pltpu.VMEM((ATTN_SKEW, tq, tk), BF16),
                        pltpu.VMEM((tk, tk), BF16)],
        compiler_params=pltpu.CompilerParams(
            dimension_semantics=("arbitrary", "arbitrary", "arbitrary"),
            vmem_limit_bytes=VMEM_LIMIT_BYTES),
        name="sb_attn",
    )(qkv, qkv, qkv, *weights_f32)
    return out[0], out[1:]


def _mix_kernel(x_ref, up_ref, halo_ref, ysb_ref, wout_ref, wpool_ref, pscale_ref,
                ling_ref, linb_ref, l1g_ref, l1b_ref, h1_ref, h1b_ref, *, seq, alpha):
    i = pl.program_id(0)
    tm = x_ref.shape[0]
    gw = wpool_ref.shape[1]
    blk_in_seq = i % (seq // tm)
    halo = jnp.where(blk_in_seq == 0, 0.0, halo_ref[...])
    u = up_ref[...]
    ext = jnp.concatenate([halo, u], axis=0)
    t = blk_in_seq * tm + lax.broadcasted_iota(jnp.int32, (tm, 1), 0)
    parts = []
    for g, w in enumerate(POOL_WINDOWS):
        cols = slice(g * gw, (g + 1) * gw)
        s = ext[:, cols]
        d = 1
        while d < w:
            s = s + pltpu.roll(s, d, axis=0)
            d *= 2
        s = s[POOL_HALO:, :]
        count = jnp.minimum(t + 1, w).astype(F32)
        y = s * (1.0 / count) - u[:, cols]
        yp = jnp.dot(y.astype(BF16), wpool_ref[g], preferred_element_type=F32)
        parts.append((yp * pscale_ref[g:g + 1, :]).astype(BF16))
    mix_in = jnp.concatenate(parts + [ysb_ref[...]], axis=1)
    mix = jnp.dot(mix_in, wout_ref[...], preferred_element_type=F32)
    h = _layer_norm(x_ref[...], ling_ref[...], linb_ref[...])
    h1 = _layer_norm(alpha * h + mix, l1g_ref[...], l1b_ref[...])
    h1_ref[...] = h1
    h1b_ref[...] = h1.astype(BF16)


def _mix_ln1(x2, u_pool, y_sb, w_out_bf16, w_pool_bf16, pool_scale, ln_in_g, ln_in_b,
             ln1_g, ln1_b, seq, alpha):
    m, d = x2.shape
    pw = u_pool.shape[1]
    sw = y_sb.shape[1]
    tm = MIX_TM
    assert seq % tm == 0 and tm % POOL_HALO == 0 and max(POOL_WINDOWS) <= POOL_HALO
    halo_blocks = tm // POOL_HALO
    const = lambda i: (0, 0)
    return pl.pallas_call(
        functools.partial(_mix_kernel, seq=seq, alpha=alpha),
        grid=(m // tm,),
        in_specs=[
            pl.BlockSpec((tm, d), lambda i: (i, 0)),
            pl.BlockSpec((tm, pw), lambda i: (i, 0)),
            pl.BlockSpec((POOL_HALO, pw), lambda i: (jnp.maximum(i * halo_blocks - 1, 0), 0)),
            pl.BlockSpec((tm, sw), lambda i: (i, 0)),
            pl.BlockSpec(w_out_bf16.shape, const, pipeline_mode=pl.Buffered(1)),
            pl.BlockSpec(w_pool_bf16.shape, lambda i: (0, 0, 0)),
            pl.BlockSpec(pool_scale.shape, const),
            pl.BlockSpec((1, d), const),
            pl.BlockSpec((1, d), const),
            pl.BlockSpec((1, d), const),
            pl.BlockSpec((1, d), const),
        ],
        out_specs=[pl.BlockSpec((tm, d), lambda i: (i, 0))] * 2,
        out_shape=[jax.ShapeDtypeStruct((m, d), F32), jax.ShapeDtypeStruct((m, d), BF16)],
        compiler_params=pltpu.CompilerParams(
            dimension_semantics=("arbitrary",),
            vmem_limit_bytes=VMEM_LIMIT_BYTES),
        name="mix_ln1",
    )(x2, u_pool, u_pool, y_sb, w_out_bf16, w_pool_bf16, pool_scale,
      ln_in_g, ln_in_b, ln1_g, ln1_b)


def _ffn_kernel(h1_ref, h1b_ref, w1_ref, b1_ref, w2_ref, b2_ref, g_ref, b_ref, o_ref,
                acc_scr, *, alpha):
    i = pl.program_id(0)
    f = pl.program_id(1)
    last_i = pl.num_programs(0) - 1
    last_f = pl.num_programs(1) - 1

    def mlp_chunk():
        t = jnp.dot(h1b_ref[...], w1_ref[...], preferred_element_type=F32) + b1_ref[...]
        t = jnp.maximum(t, 0.0)
        return jnp.dot((t * t).astype(BF16), w2_ref[...], preferred_element_type=F32)

    def norm():
        return _layer_norm(acc_scr[...], g_ref[...], b_ref[...])

    @pl.when((f == 0) & (i == 0))
    def _():
        acc_scr[...] = mlp_chunk()

    @pl.when((f == 0) & (i > 0))
    def _():
        o_ref[...] = norm()
        acc_scr[...] = mlp_chunk()

    @pl.when((f > 0) & (f < last_f))
    def _():
        acc_scr[...] += mlp_chunk()

    @pl.when(f == last_f)
    def _():
        acc_scr[...] = alpha * h1_ref[...] + (acc_scr[...] + mlp_chunk() + b2_ref[...])

    @pl.when((f == last_f) & (i == last_i))
    def _():
        o_ref[...] = norm()


def _ffn_ln2(h1, h1_bf16, w1_bf16, b1, w2_bf16, b2, g, b, alpha):
    m, d = h1.shape
    dff = w1_bf16.shape[1]
    tm, tf = FFN_TM, FFN_TF
    assert m % tm == 0 and dff % tf == 0 and dff // tf >= 2
    n_i, n_f = m // tm, dff // tf
    const = lambda i, f: (0, 0)

    def out_block(i, f):
        final = (i == n_i - 1) & (f == n_f - 1)
        return (jnp.where(final, i, jnp.maximum(i - 1, 0)), 0)

    return pl.pallas_call(
        functools.partial(_ffn_kernel, alpha=alpha),
        grid=(n_i, n_f),
        in_specs=[
            pl.BlockSpec((tm, d), lambda i, f: (i, 0)),
            pl.BlockSpec((tm, d), lambda i, f: (i, 0)),
            pl.BlockSpec((d, tf), lambda i, f: (0, f)),
            pl.BlockSpec((1, tf), lambda i, f: (0, f)),
            pl.BlockSpec((tf, d), lambda i, f: (f, 0)),
            pl.BlockSpec((1, d), const),
            pl.BlockSpec((1, d), const),
            pl.BlockSpec((1, d), const),
        ],
        out_specs=pl.BlockSpec((tm, d), out_block),
        out_shape=jax.ShapeDtypeStruct((m, d), F32),
        scratch_shapes=[pltpu.VMEM((tm, d), F32)],
        compiler_params=pltpu.CompilerParams(
            dimension_semantics=("arbitrary", "arbitrary"),
            vmem_limit_bytes=VMEM_LIMIT_BYTES),
        name="ffn_ln2",
    )(h1, h1_bf16, w1_bf16, b1, w2_bf16, b2, g, b)


def kernel(x, ln_in_g, ln_in_b, w_in, w_pool, pool_scale, w_out, ln1_g, ln1_b,
           w_ff1, b_ff1, w_ff2, b_ff2, ln2_g, ln2_b):
    batch, seq, d = x.shape
    depth, groups, gw, _ = w_pool.shape
    assert depth == 1, "single trunk layer"
    pool_width = groups * gw
    sb_width = (w_in.shape[2] - pool_width) // 3
    heads = sb_width // HEAD_DIM
    alpha = float((2.0 * depth) ** 0.25)
    m = batch * seq

    x2 = x.reshape(m, d)
    row = lambda p: p.reshape(1, -1)

    q_scale = float(LOG2_E / np.sqrt(np.float32(HEAD_DIM)))
    u_pool, qkv = _in_proj(x2, row(ln_in_g), row(ln_in_b), w_in[0], pool_width, q_scale)
    y_sb, (w_out_bf16, w_ff1_bf16, w_ff2_bf16, w_pool_bf16) = _sb_attention(
        qkv, batch, seq, heads,
        [w_out[0], w_ff1[0], w_ff2[0], w_pool[0].reshape(groups * gw, gw)])
    h1, h1_bf16 = _mix_ln1(x2, u_pool, y_sb, w_out_bf16, w_pool_bf16.reshape(groups, gw, gw),
                  pool_scale[0], row(ln_in_g), row(ln_in_b), row(ln1_g[0]), row(ln1_b[0]),
                  seq, alpha)
    out = _ffn_ln2(h1, h1_bf16, w_ff1_bf16, row(b_ff1[0]), w_ff2_bf16,
                   row(b_ff2[0]), row(ln2_g[0]), row(ln2_b[0]), alpha)
    return out.reshape(batch, seq, d)
```

```python
import functools

import jax
import jax.numpy as jnp
import numpy as np
from jax import lax
from jax.experimental import pallas as pl
from jax.experimental.pallas import tpu as pltpu

F32 = jnp.float32
BF16 = jnp.bfloat16

LN_EPS = 1e-5
LOG2_E = 1.4426950408889634
EXP2_CLAMP = 100.0
POOL_WINDOWS = (2, 4, 8, 16)
POOL_HALO = 16
HEAD_DIM = 128

VMEM_LIMIT_BYTES = 56 * 1024 * 1024

INPROJ_TM = 1024
INPROJ_TN = 1024
ATTN_TQ = 512
ATTN_TK = 256
ATTN_HEADS = 4
ATTN_SKEW = 1
ATTN_LAG = 2
MIX_TM = 512
FFN_TM = 512
FFN_TF = 1024
LN_ROWS = 128


def _layer_norm(x, g, b):
    mu = jnp.mean(x, axis=-1, keepdims=True)
    xc = x - mu
    var = jnp.mean(xc * xc, axis=-1, keepdims=True)
    return xc * lax.rsqrt(var + LN_EPS) * g + b


def _inproj_kernel(x_hbm, g_ref, b_ref, w_ref, pool_ref, qkv_ref, x_scr, h_scr, sem, *, q_scale):
    i = pl.program_id(0)
    j = pl.program_id(1)
    n_i = pl.num_programs(0)
    last_j = pl.num_programs(1) - 1
    tm = x_scr.shape[0]

    def x_copy(blk):
        return pltpu.make_async_copy(x_hbm.at[pl.ds(blk * tm, tm), :], x_scr, sem)

    def normalise(slot):
        for r in range(tm // LN_ROWS):
            rows = slice(r * LN_ROWS, (r + 1) * LN_ROWS)
            h_scr[slot, rows, :] = _layer_norm(
                x_scr[rows, :], g_ref[...], b_ref[...]).astype(BF16)

    def project(h):
        return jnp.dot(h, w_ref[...].astype(BF16), preferred_element_type=F32)

    @pl.when((i == 0) & (j == 0))
    def _():
        x_copy(0).start()
        x_copy(0).wait()
        normalise(0)

        @pl.when(n_i > 1)
        def _():
            x_copy(1).start()

    @pl.when(j < last_j)
    def _():
        acc = project(h_scr[i % 2])

        @pl.when(j == 0)
        def _():
            pool_ref[...] = acc

        @pl.when(j == 1)
        def _():
            qkv_ref[...] = (acc * q_scale).astype(BF16)

        @pl.when(j > 1)
        def _():
            qkv_ref[...] = acc.astype(BF16)

    def last_step_and_next_norm(slot):
        x_copy(i + 1).wait()
        acc = project(h_scr[slot])
        normalise(1 - slot)
        qkv_ref[...] = acc.astype(BF16)

        @pl.when(i + 2 < n_i)
        def _():
            x_copy(i + 2).start()

    for parity in (0, 1):
        @pl.when((j == last_j) & (i < n_i - 1) & (i % 2 == parity))
        def _():
            last_step_and_next_norm(parity)

    @pl.when((j == last_j) & (i == n_i - 1))
    def _():
        qkv_ref[...] = project(h_scr[i % 2]).astype(BF16)


def _in_proj(x2, g, b, w_in, pool_width, q_scale):
    m, d = x2.shape
    n = w_in.shape[1]
    tm, tn = INPROJ_TM, INPROJ_TN
    assert pool_width == tn and n == 4 * tn and m % tm == 0
    return pl.pallas_call(
        functools.partial(_inproj_kernel, q_scale=q_scale),
        grid=(m // tm, n // tn),
        in_specs=[
            pl.BlockSpec(memory_space=pl.ANY),
            pl.BlockSpec((1, d), lambda i, j: (0, 0)),
            pl.BlockSpec((1, d), lambda i, j: (0, 0)),
            pl.BlockSpec((d, tn), lambda i, j: (0, j)),
        ],
        out_specs=[
            pl.BlockSpec((tm, tn), lambda i, j: (i, 0)),
            pl.BlockSpec((tm, tn), lambda i, j: (i, jnp.maximum(j - 1, 0))),
        ],
        out_shape=[
            jax.ShapeDtypeStruct((m, pool_width), F32),
            jax.ShapeDtypeStruct((m, n - pool_width), BF16),
        ],
        scratch_shapes=[pltpu.VMEM((tm, d), F32), pltpu.VMEM((2, tm, d), BF16),
                        pltpu.SemaphoreType.DMA(())],
        compiler_params=pltpu.CompilerParams(
            dimension_semantics=("arbitrary", "arbitrary"),
            vmem_limit_bytes=VMEM_LIMIT_BYTES),
        name="in_proj",
    )(x2, g, b, w_in)


def _attn_kernel(*refs, n_cast):
    q_ref, k_ref, v_ref = refs[:3]
    cast_in = refs[3:3 + n_cast]
    o_ref = refs[3 + n_cast]
    cast_out = refs[4 + n_cast:4 + 2 * n_cast]
    carry_scr, acc_scr, w_scr, a_scr, cum_scr = refs[4 + 2 * n_cast:]
    cum_scr[...] = jnp.where(
        lax.broadcasted_iota(jnp.int32, cum_scr.shape, 0)
        >= lax.broadcasted_iota(jnp.int32, cum_scr.shape, 1), 1.0, 0.0).astype(BF16)
    for src, dst in zip(cast_in, cast_out):
        dst[...] = src[...].astype(dst.dtype)
    qi = pl.program_id(2)
    tq, tk = ATTN_TQ, ATTN_TK
    nh = q_ref.shape[1] // HEAD_DIM
    skew = ATTN_SKEW
    heads = range(nh)
    cols = [slice(h * HEAD_DIM, (h + 1) * HEAD_DIM) for h in heads]

    def key_rows(kt):
        return pl.ds(pl.multiple_of(kt * tk, tk), tk)

    def scores(kt, h):
        return lax.dot_general(q_ref[:, cols[h]], k_ref[key_rows(kt), cols[h]],
                               (((1,), (1,)), ((), ())),
                               preferred_element_type=F32)

    def deferred_values(kt, j):
        h = nh - skew + j
        acc_scr[h] += jnp.dot(a_scr[j], v_ref[key_rows(kt), cols[h]],
                              preferred_element_type=F32)

    def sweep(kts, key_offsets=None):
        first = key_offsets is not None
        n_items = len(kts) * nh
        item = lambda n: (kts[n // nh], n % nh)
        kt_after = jnp.maximum(kts[-1] - 1, 0)
        if first:
            below = (lax.broadcasted_iota(jnp.int32, (tk, tk), 1)
                     < lax.broadcasted_iota(jnp.int32, (tk, tk), 0))

        def masked(n, x):
            off = key_offsets[n // nh]
            parts = []
            for r in range(tq // tk):
                rows = x[r * tk:(r + 1) * tk]
                if r * tk < off:
                    parts.append(jnp.zeros_like(rows))
                elif r * tk == off:
                    parts.append(jnp.where(below, rows, 0.0))
                else:
                    parts.append(rows)
            return jnp.concatenate(parts, axis=0)

        def softplus_tile(n, w):
            sp = jnp.maximum(
                w, jnp.log(1.0 + jnp.exp2(jnp.minimum(w, EXP2_CLAMP))) * LOG2_E)
            if first:
                sp = masked(n, sp)
            return sp.astype(BF16)

        def weights_values(n, w, cs):
            kt, h = item(n)
            start = first and n < nh
            total = jnp.broadcast_to(cs[:, 0:1], (tq, HEAD_DIM))
            if start:
                a = jnp.exp2(w - cs)
            else:
                carry = carry_scr[h]
                a = jnp.exp2(w - cs - jnp.concatenate([carry] * (tk // HEAD_DIM), axis=1))
            if first:
                a = masked(n, a)
            carry_scr[h] = total if start else carry + total
            if n >= n_items - skew:
                a_scr[n - (n_items - skew)] = a.astype(BF16)
                if start:
                    acc_scr[h] = jnp.zeros((tq, HEAD_DIM), F32)
                return
            av = jnp.dot(a.astype(BF16), v_ref[key_rows(kt), cols[h]],
                         preferred_element_type=F32)
            if start:
                acc_scr[h] = av
            else:
                acc_scr[h] += av

        w, cs, w_after = {}, {}, {}
        for j in range(skew):
            w[j] = scores(*item(j)) if first else w_scr[j]
        for n in range(n_items):
            if n + skew < n_items:
                w[n + skew] = scores(*item(n + skew))
            else:
                w_after[n + skew - n_items] = scores(kt_after, n + skew - n_items)
            cs[n] = jnp.dot(softplus_tile(n, w[n]), cum_scr[...], preferred_element_type=F32)
            if n >= ATTN_LAG:
                weights_values(n - ATTN_LAG, w.pop(n - ATTN_LAG), cs.pop(n - ATTN_LAG))
            if n < skew and not first:
                deferred_values(kts[0] + 1, n)
        for n in range(n_items - ATTN_LAG, n_items):
            weights_values(n, w.pop(n), cs.pop(n))
        for j in range(skew):
            w_scr[j] = w_after[j]

    top = (qi + 1) * (tq // tk) - 1
    sweep([top, top - 1], [tk, 0])

    def pair_body(n, _):
        kt = top - 2 - 2 * n
        sweep([kt, kt - 1])
        return 0

    lax.fori_loop(0, qi, pair_body, 0)
    for j in range(skew):
        deferred_values(0, j)
    for h in heads:
        o_ref[:, cols[h]] = acc_scr[h].astype(o_ref.dtype)


def _sb_attention(qkv, batch, seq, heads, weights_f32):
    m = qkv.shape[0]
    nh = ATTN_HEADS
    tq, tk = ATTN_TQ, ATTN_TK
    assert heads % nh == 0 and seq % tq == 0 and tq == 2 * tk
    groups = heads // nh
    nq = seq // tq
    width = nh * HEAD_DIM
    steps = batch * groups * nq
    step = lambda b, g, i: ((b * groups + g) * nq + i, 0)
    slabs = []
    for wgt in weights_f32:
        rows, ncol = wgt.shape
        assert rows % (steps * 16) == 0
        slabs.append(pl.BlockSpec((rows // steps, ncol), step))
    out = pl.pallas_call(
        functools.partial(_attn_kernel, n_cast=len(weights_f32)),
        grid=(batch, groups, nq),
        in_specs=[
            pl.BlockSpec((tq, width), lambda b, g, i: (b * nq + i, g)),
            pl.BlockSpec((seq, width), lambda b, g, i: (b, groups + g)),
            pl.BlockSpec((seq, width), lambda b, g, i: (b, 2 * groups + g)),
        ] + slabs,
        out_specs=[pl.BlockSpec((tq, width), lambda b, g, i: (b * nq + i, g))] + slabs,
        out_shape=[jax.ShapeDtypeStruct((m, heads * HEAD_DIM), BF16)]
        + [jax.ShapeDtypeStruct(wgt.shape, BF16) for wgt in weights_f32],
        scratch_shapes=[pltpu.VMEM((nh, tq, HEAD_DIM), F32), pltpu.VMEM((nh, tq, HEAD_DIM), F32),
                        pltpu.VMEM((ATTN_SKEW, tq, tk), F32), pltpu.VMEM((ATTN_SKEW, tq, tk), BF16),
                        pltpu.VMEM((tk, tk), BF16)],
        compiler_params=pltpu.CompilerParams(
            dimension_semantics=("arbitrary", "arbitrary", "arbitrary"),
            vmem_limit_bytes=VMEM_LIMIT_BYTES),
        name="sb_attn",
    )(qkv, qkv, qkv, *weights_f32)
    return out[0], out[1:]


def _mix_kernel(x_ref, up_ref, halo_ref, ysb_ref, wout_ref, wpool_ref, pscale_ref,
                ling_ref, linb_ref, l1g_ref, l1b_ref, h1_ref, h1b_ref, *, seq, alpha):
    i = pl.program_id(0)
    tm = x_ref.shape[0]
    gw = wpool_ref.shape[1]
    blk_in_seq = i % (seq // tm)
    halo = jnp.where(blk_in_seq == 0, 0.0, halo_ref[...])
    u = up_ref[...]
    ext = jnp.concatenate([halo, u], axis=0)
    t = blk_in_seq * tm + lax.broadcasted_iota(jnp.int32, (tm, 1), 0)
    parts = []
    for g, w in enumerate(POOL_WINDOWS):
        cols = slice(g * gw, (g + 1) * gw)
        s = ext[:, cols]
        d = 1
        while d < w:
            s = s + pltpu.roll(s, d, axis=0)
            d *= 2
        s = s[POOL_HALO:, :]
        count = jnp.minimum(t + 1, w).astype(F32)
        y = s * (1.0 / count) - u[:, cols]
        yp = jnp.dot(y.astype(BF16), wpool_ref[g], preferred_element_type=F32)
        parts.append((yp * pscale_ref[g:g + 1, :]).astype(BF16))
    mix_in = jnp.concatenate(parts + [ysb_ref[...]], axis=1)
    mix = jnp.dot(mix_in, wout_ref[...], preferred_element_type=F32)
    h = _layer_norm(x_ref[...], ling_ref[...], linb_ref[...])
    h1 = _layer_norm(alpha * h + mix, l1g_ref[...], l1b_ref[...])
    h1_ref[...] = h1
    h1b_ref[...] = h1.astype(BF16)


def _mix_ln1(x2, u_pool, y_sb, w_out_bf16, w_pool_bf16, pool_scale, ln_in_g, ln_in_b,
             ln1_g, ln1_b, seq, alpha):
    m, d = x2.shape
    pw = u_pool.shape[1]
    sw = y_sb.shape[1]
    tm = MIX_TM
    assert seq % tm == 0 and tm % POOL_HALO == 0 and max(POOL_WINDOWS) <= POOL_HALO
    halo_blocks = tm // POOL_HALO
    const = lambda i: (0, 0)
    return pl.pallas_call(
        functools.partial(_mix_kernel, seq=seq, alpha=alpha),
        grid=(m // tm,),
        in_specs=[
            pl.BlockSpec((tm, d), lambda i: (i, 0)),
            pl.BlockSpec((tm, pw), lambda i: (i, 0)),
            pl.BlockSpec((POOL_HALO, pw), lambda i: (jnp.maximum(i * halo_blocks - 1, 0), 0)),
            pl.BlockSpec((tm, sw), lambda i: (i, 0)),
            pl.BlockSpec(w_out_bf16.shape, const, pipeline_mode=pl.Buffered(1)),
            pl.BlockSpec(w_pool_bf16.shape, lambda i: (0, 0, 0)),
            pl.BlockSpec(pool_scale.shape, const),
            pl.BlockSpec((1, d), const),
            pl.BlockSpec((1, d), const),
            pl.BlockSpec((1, d), const),
            pl.BlockSpec((1, d), const),
        ],
        out_specs=[pl.BlockSpec((tm, d), lambda i: (i, 0))] * 2,
        out_shape=[jax.ShapeDtypeStruct((m, d), F32), jax.ShapeDtypeStruct((m, d), BF16)],
        compiler_params=pltpu.CompilerParams(
            dimension_semantics=("arbitrary",),
            vmem_limit_bytes=VMEM_LIMIT_BYTES),
        name="mix_ln1",
    )(x2, u_pool, u_pool, y_sb, w_out_bf16, w_pool_bf16, pool_scale,
      ln_in_g, ln_in_b, ln1_g, ln1_b)


def _ffn_kernel(h1_ref, h1b_ref, w1_ref, b1_ref, w2_ref, b2_ref, g_ref, b_ref, o_ref,
                acc_scr, *, alpha):
    i = pl.program_id(0)
    f = pl.program_id(1)
    last_i = pl.num_programs(0) - 1
    last_f = pl.num_programs(1) - 1

    def mlp_chunk():
        t = jnp.dot(h1b_ref[...], w1_ref[...], preferred_element_type=F32) + b1_ref[...]
        t = jnp.maximum(t, 0.0)
        return jnp.dot((t * t).astype(BF16), w2_ref[...], preferred_element_type=F32)

    def norm():
        return _layer_norm(acc_scr[...], g_ref[...], b_ref[...])

    @pl.when((f == 0) & (i == 0))
    def _():
        acc_scr[...] = mlp_chunk()

    @pl.when((f == 0) & (i > 0))
    def _():
        o_ref[...] = norm()
        acc_scr[...] = mlp_chunk()

    @pl.when((f > 0) & (f < last_f))
    def _():
        acc_scr[...] += mlp_chunk()

    @pl.when(f == last_f)
    def _():
        acc_scr[...] = alpha * h1_ref[...] + (acc_scr[...] + mlp_chunk() + b2_ref[...])

    @pl.when((f == last_f) & (i == last_i))
    def _():
        o_ref[...] = norm()


def _ffn_ln2(h1, h1_bf16, w1_bf16, b1, w2_bf16, b2, g, b, alpha):
    m, d = h1.shape
    dff = w1_bf16.shape[1]
    tm, tf = FFN_TM, FFN_TF
    assert m % tm == 0 and dff % tf == 0 and dff // tf >= 2
    n_i, n_f = m // tm, dff // tf
    const = lambda i, f: (0, 0)

    def out_block(i, f):
        final = (i == n_i - 1) & (f == n_f - 1)
        return (jnp.where(final, i, jnp.maximum(i - 1, 0)), 0)

    return pl.pallas_call(
        functools.partial(_ffn_kernel, alpha=alpha),
        grid=(n_i, n_f),
        in_specs=[
            pl.BlockSpec((tm, d), lambda i, f: (i, 0)),
            pl.BlockSpec((tm, d), lambda i, f: (i, 0)),
            pl.BlockSpec((d, tf), lambda i, f: (0, f)),
            pl.BlockSpec((1, tf), lambda i, f: (0, f)),
            pl.BlockSpec((tf, d), lambda i, f: (f, 0)),
            pl.BlockSpec((1, d), const),
            pl.BlockSpec((1, d), const),
            pl.BlockSpec((1, d), const),
        ],
        out_specs=pl.BlockSpec((tm, d), out_block),
        out_shape=jax.ShapeDtypeStruct((m, d), F32),
        scratch_shapes=[pltpu.VMEM((tm, d), F32)],
        compiler_params=pltpu.CompilerParams(
            dimension_semantics=("arbitrary", "arbitrary"),
            vmem_limit_bytes=VMEM_LIMIT_BYTES),
        name="ffn_ln2",
    )(h1, h1_bf16, w1_bf16, b1, w2_bf16, b2, g, b)


def kernel(x, ln_in_g, ln_in_b, w_in, w_pool, pool_scale, w_out, ln1_g, ln1_b,
           w_ff1, b_ff1, w_ff2, b_ff2, ln2_g, ln2_b):
    batch, seq, d = x.shape
    depth, groups, gw, _ = w_pool.shape
    assert depth == 1, "single trunk layer"
    pool_width = groups * gw
    sb_width = (w_in.shape[2] - pool_width) // 3
    heads = sb_width // HEAD_DIM
    alpha = float((2.0 * depth) ** 0.25)
    m = batch * seq

    x2 = x.reshape(m, d)
    row = lambda p: p.reshape(1, -1)

    q_scale = float(LOG2_E / np.sqrt(np.float32(HEAD_DIM)))
    u_pool, qkv = _in_proj(x2, row(ln_in_g), row(ln_in_b), w_in[0], pool_width, q_scale)
    y_sb, (w_out_bf16, w_ff1_bf16, w_ff2_bf16, w_pool_bf16) = _sb_attention(
        qkv, batch, seq, heads,
        [w_out[0], w_ff1[0], w_ff2[0], w_pool[0].reshape(groups * gw, gw)])
    h1, h1_bf16 = _mix_ln1(x2, u_pool, y_sb, w_out_bf16, w_pool_bf16.reshape(groups, gw, gw),
                  pool_scale[0], row(ln_in_g), row(ln_in_b), row(ln1_g[0]), row(ln1_b[0]),
                  seq, alpha)
    out = _ffn_ln2(h1, h1_bf16, w_ff1_bf16, row(b_ff1[0]), w_ff2_bf16,
                   row(b_ff2[0]), row(ln2_g[0]), row(ln2_b[0]), alpha)
    return out.reshape(batch, seq, d)
```

```python
import functools

import jax
import jax.numpy as jnp
import numpy as np
from jax import lax
from jax.experimental import pallas as pl
from jax.experimental.pallas import tpu as pltpu

F32 = jnp.float32
BF16 = jnp.bfloat16

LN_EPS = 1e-5
LOG2_E = 1.4426950408889634
EXP2_CLAMP = 100.0
POOL_WINDOWS = (2, 4, 8, 16)
POOL_HALO = 16
HEAD_DIM = 128

VMEM_LIMIT_BYTES = 56 * 1024 * 1024

INPROJ_TM = 1024
INPROJ_TN = 1024
ATTN_TQ = 512
ATTN_TK = 256
ATTN_HEADS = 4
ATTN_SKEW = 1
ATTN_LAG = 2
MIX_TM = 512
FFN_TM = 512
FFN_TF = 1024
LN_ROWS = 128


def _layer_norm(x, g, b):
    mu = jnp.mean(x, axis=-1, keepdims=True)
    xc = x - mu
    var = jnp.mean(xc * xc, axis=-1, keepdims=True)
    return xc * lax.rsqrt(var + LN_EPS) * g + b


def _inproj_kernel(x_hbm, g_ref, b_ref, w_ref, pool_ref, qkv_ref, x_scr, h_scr, sem, *, q_scale):
    i = pl.program_id(0)
    j = pl.program_id(1)
    n_i = pl.num_programs(0)
    last_j = pl.num_programs(1) - 1
    tm = x_scr.shape[0]

    def x_copy(blk):
        return pltpu.make_async_copy(x_hbm.at[pl.ds(blk * tm, tm), :], x_scr, sem)

    def normalise(slot):
        for r in range(tm // LN_ROWS):
            rows = slice(r * LN_ROWS, (r + 1) * LN_ROWS)
            h_scr[slot, rows, :] = _layer_norm(
                x_scr[rows, :], g_ref[...], b_ref[...]).astype(BF16)

    def project(h):
        return jnp.dot(h, w_ref[...].astype(BF16), preferred_element_type=F32)

    @pl.when((i == 0) & (j == 0))
    def _():
        x_copy(0).start()
        x_copy(0).wait()
        normalise(0)

        @pl.when(n_i > 1)
        def _():
            x_copy(1).start()

    @pl.when(j < last_j)
    def _():
        acc = project(h_scr[i % 2])

        @pl.when(j == 0)
        def _():
            pool_ref[...] = acc

        @pl.when(j == 1)
        def _():
            qkv_ref[...] = (acc * q_scale).astype(BF16)

        @pl.when(j > 1)
        def _():
            qkv_ref[...] = acc.astype(BF16)

    def last_step_and_next_norm(slot):
        x_copy(i + 1).wait()
        acc = project(h_scr[slot])
        normalise(1 - slot)
        qkv_ref[...] = acc.astype(BF16)

        @pl.when(i + 2 < n_i)
        def _():
            x_copy(i + 2).start()

    for parity in (0, 1):
        @pl.when((j == last_j) & (i < n_i - 1) & (i % 2 == parity))
        def _():
            last_step_and_next_norm(parity)

    @pl.when((j == last_j) & (i == n_i - 1))
    def _():
        qkv_ref[...] = project(h_scr[i % 2]).astype(BF16)


def _in_proj(x2, g, b, w_in, pool_width, q_scale):
    m, d = x2.shape
    n = w_in.shape[1]
    tm, tn = INPROJ_TM, INPROJ_TN
    assert pool_width == tn and n == 4 * tn and m % tm == 0
    return pl.pallas_call(
        functools.partial(_inproj_kernel, q_scale=q_scale),
        grid=(m // tm, n // tn),
        in_specs=[
            pl.BlockSpec(memory_space=pl.ANY),
            pl.BlockSpec((1, d), lambda i, j: (0, 0)),
            pl.BlockSpec((1, d), lambda i, j: (0, 0)),
            pl.BlockSpec((d, tn), lambda i, j: (0, j)),
        ],
        out_specs=[
            pl.BlockSpec((tm, tn), lambda i, j: (i, 0)),
            pl.BlockSpec((tm, tn), lambda i, j: (i, jnp.maximum(j - 1, 0))),
        ],
        out_shape=[
            jax.ShapeDtypeStruct((m, pool_width), F32),
            jax.ShapeDtypeStruct((m, n - pool_width), BF16),
        ],
        scratch_shapes=[pltpu.VMEM((tm, d), F32), pltpu.VMEM((2, tm, d), BF16),
                        pltpu.SemaphoreType.DMA(())],
        compiler_params=pltpu.CompilerParams(
            dimension_semantics=("arbitrary", "arbitrary"),
            vmem_limit_bytes=VMEM_LIMIT_BYTES),
        name="in_proj",
    )(x2, g, b, w_in)


def _attn_kernel(*refs, n_cast):
    q_ref, k_ref, v_ref = refs[:3]
    cast_in = refs[3:3 + n_cast]
    o_ref = refs[3 + n_cast]
    cast_out = refs[4 + n_cast:4 + 2 * n_cast]
    carry_scr, acc_scr, w_scr, a_scr, cum_scr = refs[4 + 2 * n_cast:]
    cum_scr[...] = jnp.where(
        lax.broadcasted_iota(jnp.int32, cum_scr.shape, 0)
        >= lax.broadcasted_iota(jnp.int32, cum_scr.shape, 1), 1.0, 0.0).astype(BF16)
    for src, dst in zip(cast_in, cast_out):
        dst[...] = src[...].astype(dst.dtype)
    qi = pl.program_id(2)
    tq, tk = ATTN_TQ, ATTN_TK
    nh = q_ref.shape[1] // HEAD_DIM
    skew = ATTN_SKEW
    heads = range(nh)
    cols = [slice(h * HEAD_DIM, (h + 1) * HEAD_DIM) for h in heads]

    def key_rows(kt):
        return pl.ds(pl.multiple_of(kt * tk, tk), tk)

    def scores(kt, h, q_rows=slice(None)):
        return lax.dot_general(q_ref[q_rows, cols[h]], k_ref[key_rows(kt), cols[h]],
                               (((1,), (1,)), ((), ())),
                               preferred_element_type=F32)

    def deferred_values(kt, j):
        h = nh - skew + j
        acc_scr[h] += jnp.dot(a_scr[j], v_ref[key_rows(kt), cols[h]],
                              preferred_element_type=F32)

    def sweep(kts, key_offsets=None):
        first = key_offsets is not None
        n_items = len(kts) * nh
        item = lambda n: (kts[n // nh], n % nh)
        kt_after = jnp.maximum(kts[-1] - 1, 0)
        if first:
            below = (lax.broadcasted_iota(jnp.int32, (tk, tk), 1)
                     < lax.broadcasted_iota(jnp.int32, (tk, tk), 0))

        def live_rows(n):
            return slice(key_offsets[n // nh], tq) if first else slice(None)

        def masked(n, x):
            off = key_offsets[n // nh]
            parts = []
            for r in range(off // tk, tq // tk):
                rows = x[r * tk - off:(r + 1) * tk - off]
                if r * tk == off:
                    parts.append(jnp.where(below, rows, 0.0))
                else:
                    parts.append(rows)
            return jnp.concatenate(parts, axis=0)

        def full_rows(n, x):
            off = key_offsets[n // nh] if first else 0
            if off == 0:
                return x
            return jnp.concatenate([jnp.zeros((off, x.shape[1]), x.dtype), x], axis=0)

        def softplus_tile(n, w):
            sp = jnp.maximum(
                w, jnp.log(1.0 + jnp.exp2(jnp.minimum(w, EXP2_CLAMP))) * LOG2_E)
            if first:
                sp = masked(n, sp)
            return sp.astype(BF16)

        def weights_values(n, w, cs):
            kt, h = item(n)
            start = first and n < nh
            total = full_rows(n, jnp.broadcast_to(cs[:, 0:1], (cs.shape[0], HEAD_DIM)))
            if start:
                a = jnp.exp2(w - cs)
            else:
                carry = carry_scr[h]
                a = jnp.exp2(w - cs - jnp.concatenate([carry] * (tk // HEAD_DIM), axis=1))
            if first:
                a = masked(n, a)
            carry_scr[h] = total if start else carry + total
            if n >= n_items - skew:
                a_scr[n - (n_items - skew)] = full_rows(n, a.astype(BF16))
                if start:
                    acc_scr[h] = jnp.zeros((tq, HEAD_DIM), F32)
                return
            av = full_rows(n, jnp.dot(a.astype(BF16), v_ref[key_rows(kt), cols[h]],
                                      preferred_element_type=F32))
            if start:
                acc_scr[h] = av
            else:
                acc_scr[h] += av

        w, cs, w_after = {}, {}, {}
        for j in range(skew):
            w[j] = scores(*item(j), live_rows(j)) if first else w_scr[j]
        for n in range(n_items):
            if n + skew < n_items:
                w[n + skew] = scores(*item(n + skew), live_rows(n + skew))
            else:
                w_after[n + skew - n_items] = scores(kt_after, n + skew - n_items)
            cs[n] = jnp.dot(softplus_tile(n, w[n]), cum_scr[...], preferred_element_type=F32)
            if n >= ATTN_LAG:
                weights_values(n - ATTN_LAG, w.pop(n - ATTN_LAG), cs.pop(n - ATTN_LAG))
            if n < skew and not first:
                deferred_values(kts[0] + 1, n)
        for n in range(n_items - ATTN_LAG, n_items):
            weights_values(n, w.pop(n), cs.pop(n))
        for j in range(skew):
            w_scr[j] = w_after[j]

    top = (qi + 1) * (tq // tk) - 1
    sweep([top, top - 1], [tk, 0])

    def pair_body(n, _):
        kt = top - 2 - 2 * n
        sweep([kt, kt - 1])
        return 0

    lax.fori_loop(0, qi, pair_body, 0)
    for j in range(skew):
        deferred_values(0, j)
    for h in heads:
        o_ref[:, cols[h]] = acc_scr[h].astype(o_ref.dtype)


def _sb_attention(qkv, batch, seq, heads, weights_f32):
    m = qkv.shape[0]
    nh = ATTN_HEADS
    tq, tk = ATTN_TQ, ATTN_TK
    assert heads % nh == 0 and seq % tq == 0 and tq == 2 * tk
    groups = heads // nh
    nq = seq // tq
    width = nh * HEAD_DIM
    steps = batch * groups * nq
    step = lambda b, g, i: ((b * groups + g) * nq + i, 0)
    slabs = []
    for wgt in weights_f32:
        rows, ncol = wgt.shape
        assert rows % (steps * 16) == 0
        slabs.append(pl.BlockSpec((rows // steps, ncol), step))
    out = pl.pallas_call(
        functools.partial(_attn_kernel, n_cast=len(weights_f32)),
        grid=(batch, groups, nq),
        in_specs=[
            pl.BlockSpec((tq, width), lambda b, g, i: (b * nq + i, g)),
            pl.BlockSpec((seq, width), lambda b, g, i: (b, groups + g)),
            pl.BlockSpec((seq, width), lambda b, g, i: (b, 2 * groups + g)),
        ] + slabs,
        out_specs=[pl.BlockSpec((tq, width), lambda b, g, i: (b * nq + i, g))] + slabs,
        out_shape=[jax.ShapeDtypeStruct((m, heads * HEAD_DIM), BF16)]
        + [jax.ShapeDtypeStruct(wgt.shape, BF16) for wgt in weights_f32],
        scratch_shapes=[pltpu.VMEM((nh, tq, HEAD_DIM), F32), pltpu.VMEM((nh, tq, HEAD_DIM), F32),
                        pltpu.VMEM((ATTN_SKEW, tq, tk), F32), pltpu.VMEM((ATTN_SKEW, tq, tk), BF16),
                        pltpu.VMEM((tk, tk), BF16)],
        compiler_params=pltpu.CompilerParams(
            dimension_semantics=("arbitrary", "arbitrary", "arbitrary"),
            vmem_limit_bytes=VMEM_LIMIT_BYTES),
        name="sb_attn",
    )(qkv, qkv, qkv, *weights_f32)
    return out[0], out[1:]


def _mix_kernel(x_ref, up_ref, halo_ref, ysb_ref, wout_ref, wpool_ref, pscale_ref,
                ling_ref, linb_ref, l1g_ref, l1b_ref, h1_ref, h1b_ref, *, seq, alpha):
    i = pl.program_id(0)
    tm = x_ref.shape[0]
    gw = wpool_ref.shape[1]
    blk_in_seq = i % (seq // tm)
    halo = jnp.where(blk_in_seq == 0, 0.0, halo_ref[...])
    u = up_ref[...]
    ext = jnp.concatenate([halo, u], axis=0)
    t = blk_in_seq * tm + lax.broadcasted_iota(jnp.int32, (tm, 1), 0)
    parts = []
    for g, w in enumerate(POOL_WINDOWS):
        cols = slice(g * gw, (g + 1) * gw)
        s = ext[:, cols]
        d = 1
        while d < w:
            s = s + pltpu.roll(s, d, axis=0)
            d *= 2
        s = s[POOL_HALO:, :]
        count = jnp.minimum(t + 1, w).astype(F32)
        y = s * (1.0 / count) - u[:, cols]
        yp = jnp.dot(y.astype(BF16), wpool_ref[g], preferred_element_type=F32)
        parts.append((yp * pscale_ref[g:g + 1, :]).astype(BF16))
    mix_in = jnp.concatenate(parts + [ysb_ref[...]], axis=1)
    mix = jnp.dot(mix_in, wout_ref[...], preferred_element_type=F32)
    h = _layer_norm(x_ref[...], ling_ref[...], linb_ref[...])
    h1 = _layer_norm(alpha * h + mix, l1g_ref[...], l1b_ref[...])
    h1_ref[...] = h1
    h1b_ref[...] = h1.astype(BF16)


def _mix_ln1(x2, u_pool, y_sb, w_out_bf16, w_pool_bf16, pool_scale, ln_in_g, ln_in_b,
             ln1_g, ln1_b, seq, alpha):
    m, d = x2.shape
    pw = u_pool.shape[1]
    sw = y_sb.shape[1]
    tm = MIX_TM
    assert seq % tm == 0 and tm % POOL_HALO == 0 and max(POOL_WINDOWS) <= POOL_HALO
    halo_blocks = tm // POOL_HALO
    const = lambda i: (0, 0)
    return pl.pallas_call(
        functools.partial(_mix_kernel, seq=seq, alpha=alpha),
        grid=(m // tm,),
        in_specs=[
            pl.BlockSpec((tm, d), lambda i: (i, 0)),
            pl.BlockSpec((tm, pw), lambda i: (i, 0)),
            pl.BlockSpec((POOL_HALO, pw), lambda i: (jnp.maximum(i * halo_blocks - 1, 0), 0)),
            pl.BlockSpec((tm, sw), lambda i: (i, 0)),
            pl.BlockSpec(w_out_bf16.shape, const, pipeline_mode=pl.Buffered(1)),
            pl.BlockSpec(w_pool_bf16.shape, lambda i: (0, 0, 0)),
            pl.BlockSpec(pool_scale.shape, const),
            pl.BlockSpec((1, d), const),
            pl.BlockSpec((1, d), const),
            pl.BlockSpec((1, d), const),
            pl.BlockSpec((1, d), const),
        ],
        out_specs=[pl.BlockSpec((tm, d), lambda i: (i, 0))] * 2,
        out_shape=[jax.ShapeDtypeStruct((m, d), F32), jax.ShapeDtypeStruct((m, d), BF16)],
        compiler_params=pltpu.CompilerParams(
            dimension_semantics=("arbitrary",),
            vmem_limit_bytes=VMEM_LIMIT_BYTES),
        name="mix_ln1",
    )(x2, u_pool, u_pool, y_sb, w_out_bf16, w_pool_bf16, pool_scale,
      ln_in_g, ln_in_b, ln1_g, ln1_b)


def _ffn_kernel(h1_ref, h1b_ref, w1_ref, b1_ref, w2_ref, b2_ref, g_ref, b_ref, o_ref,
                acc_scr, *, alpha):
    i = pl.program_id(0)
    f = pl.program_id(1)
    last_i = pl.num_programs(0) - 1
    last_f = pl.num_programs(1) - 1

    def mlp_chunk():
        t = jnp.dot(h1b_ref[...], w1_ref[...], preferred_element_type=F32) + b1_ref[...]
        t = jnp.maximum(t, 0.0)
        return jnp.dot((t * t).astype(BF16), w2_ref[...], preferred_element_type=F32)

    def norm():
        return _layer_norm(acc_scr[...], g_ref[...], b_ref[...])

    @pl.when((f == 0) & (i == 0))
    def _():
        acc_scr[...] = mlp_chunk()

    @pl.when((f == 0) & (i > 0))
    def _():
        o_ref[...] = norm()
        acc_scr[...] = mlp_chunk()

    @pl.when((f > 0) & (f < last_f))
    def _():
        acc_scr[...] += mlp_chunk()

    @pl.when(f == last_f)
    def _():
        acc_scr[...] = alpha * h1_ref[...] + (acc_scr[...] + mlp_chunk() + b2_ref[...])

    @pl.when((f == last_f) & (i == last_i))
    def _():
        o_ref[...] = norm()


def _ffn_ln2(h1, h1_bf16, w1_bf16, b1, w2_bf16, b2, g, b, alpha):
    m, d = h1.shape
    dff = w1_bf16.shape[1]
    tm, tf = FFN_TM, FFN_TF
    assert m % tm == 0 and dff % tf == 0 and dff // tf >= 2
    n_i, n_f = m // tm, dff // tf
    const = lambda i, f: (0, 0)

    def out_block(i, f):
        final = (i == n_i - 1) & (f == n_f - 1)
        return (jnp.where(final, i, jnp.maximum(i - 1, 0)), 0)

    return pl.pallas_call(
        functools.partial(_ffn_kernel, alpha=alpha),
        grid=(n_i, n_f),
        in_specs=[
            pl.BlockSpec((tm, d), lambda i, f: (i, 0)),
            pl.BlockSpec((tm, d), lambda i, f: (i, 0)),
            pl.BlockSpec((d, tf), lambda i, f: (0, f)),
            pl.BlockSpec((1, tf), lambda i, f: (0, f)),
            pl.BlockSpec((tf, d), lambda i, f: (f, 0)),
            pl.BlockSpec((1, d), const),
            pl.BlockSpec((1, d), const),
            pl.BlockSpec((1, d), const),
        ],
        out_specs=pl.BlockSpec((tm, d), out_block),
        out_shape=jax.ShapeDtypeStruct((m, d), F32),
        scratch_shapes=[pltpu.VMEM((tm, d), F32)],
        compiler_params=pltpu.CompilerParams(
            dimension_semantics=("arbitrary", "arbitrary"),
            vmem_limit_bytes=VMEM_LIMIT_BYTES),
        name="ffn_ln2",
    )(h1, h1_bf16, w1_bf16, b1, w2_bf16, b2, g, b)


def kernel(x, ln_in_g, ln_in_b, w_in, w_pool, pool_scale, w_out, ln1_g, ln1_b,
           w_ff1, b_ff1, w_ff2, b_ff2, ln2_g, ln2_b):
    batch, seq, d = x.shape
    depth, groups, gw, _ = w_pool.shape
    assert depth == 1, "single trunk layer"
    pool_width = groups * gw
    sb_width = (w_in.shape[2] - pool_width) // 3
    heads = sb_width // HEAD_DIM
    alpha = float((2.0 * depth) ** 0.25)
    m = batch * seq

    x2 = x.reshape(m, d)
    row = lambda p: p.reshape(1, -1)

    q_scale = float(LOG2_E / np.sqrt(np.float32(HEAD_DIM)))
    u_pool, qkv = _in_proj(x2, row(ln_in_g), row(ln_in_b), w_in[0], pool_width, q_scale)
    y_sb, (w_out_bf16, w_ff1_bf16, w_ff2_bf16, w_pool_bf16) = _sb_attention(
        qkv, batch, seq, heads,
        [w_out[0], w_ff1[0], w_ff2[0], w_pool[0].reshape(groups * gw, gw)])
    h1, h1_bf16 = _mix_ln1(x2, u_pool, y_sb, w_out_bf16, w_pool_bf16.reshape(groups, gw, gw),
                  pool_scale[0], row(ln_in_g), row(ln_in_b), row(ln1_g[0]), row(ln1_b[0]),
                  seq, alpha)
    out = _ffn_ln2(h1, h1_bf16, w_ff1_bf16, row(b_ff1[0]), w_ff2_bf16,
                   row(b_ff2[0]), row(ln2_g[0]), row(ln2_b[0]), alpha)
    return out.reshape(batch, seq, d)
```

```python
import functools

import jax
import jax.numpy as jnp
import numpy as np
from jax import lax
from jax.experimental import pallas as pl
from jax.experimental.pallas import tpu as pltpu

F32 = jnp.float32
BF16 = jnp.bfloat16

LN_EPS = 1e-5
LOG2_E = 1.4426950408889634
EXP2_CLAMP = 100.0
POOL_WINDOWS = (2, 4, 8, 16)
POOL_HALO = 16
HEAD_DIM = 128

VMEM_LIMIT_BYTES = 56 * 1024 * 1024

INPROJ_TM = 1024
INPROJ_TN = 1024
ATTN_TQ = 512
ATTN_TK = 256
ATTN_HEADS = 4
ATTN_SKEW = 1
ATTN_LAG = 2
MIX_TM = 512
FFN_TM = 512
FFN_TF = 1024
LN_ROWS = 128


def _layer_norm(x, g, b):
    mu = jnp.mean(x, axis=-1, keepdims=True)
    xc = x - mu
    var = jnp.mean(xc * xc, axis=-1, keepdims=True)
    return xc * lax.rsqrt(var + LN_EPS) * g + b


def _inproj_kernel(x_hbm, g_ref, b_ref, w_ref, pool_ref, qkv_ref, x_scr, h_scr, sem, *, q_scale):
    i = pl.program_id(0)
    j = pl.program_id(1)
    n_i = pl.num_programs(0)
    last_j = pl.num_programs(1) - 1
    tm = x_scr.shape[0]

    def x_copy(blk):
        return pltpu.make_async_copy(x_hbm.at[pl.ds(blk * tm, tm), :], x_scr, sem)

    def normalise(slot):
        for r in range(tm // LN_ROWS):
            rows = slice(r * LN_ROWS, (r + 1) * LN_ROWS)
            h_scr[slot, rows, :] = _layer_norm(
                x_scr[rows, :], g_ref[...], b_ref[...]).astype(BF16)

    def project(h):
        return jnp.dot(h, w_ref[...].astype(BF16), preferred_element_type=F32)

    @pl.when((i == 0) & (j == 0))
    def _():
        x_copy(0).start()
        x_copy(0).wait()
        normalise(0)

        @pl.when(n_i > 1)
        def _():
            x_copy(1).start()

    @pl.when(j < last_j)
    def _():
        acc = project(h_scr[i % 2])

        @pl.when(j == 0)
        def _():
            pool_ref[...] = acc

        @pl.when(j == 1)
        def _():
            qkv_ref[...] = (acc * q_scale).astype(BF16)

        @pl.when(j > 1)
        def _():
            qkv_ref[...] = acc.astype(BF16)

    def last_step_and_next_norm(slot):
        x_copy(i + 1).wait()
        acc = project(h_scr[slot])
        normalise(1 - slot)
        qkv_ref[...] = acc.astype(BF16)

        @pl.when(i + 2 < n_i)
        def _():
            x_copy(i + 2).start()

    for parity in (0, 1):
        @pl.when((j == last_j) & (i < n_i - 1) & (i % 2 == parity))
        def _():
            last_step_and_next_norm(parity)

    @pl.when((j == last_j) & (i == n_i - 1))
    def _():
        qkv_ref[...] = project(h_scr[i % 2]).astype(BF16)


def _in_proj(x2, g, b, w_in, pool_width, q_scale):
    m, d = x2.shape
    n = w_in.shape[1]
    tm, tn = INPROJ_TM, INPROJ_TN
    assert pool_width == tn and n == 4 * tn and m % tm == 0
    return pl.pallas_call(
        functools.partial(_inproj_kernel, q_scale=q_scale),
        grid=(m // tm, n // tn),
        in_specs=[
            pl.BlockSpec(memory_space=pl.ANY),
            pl.BlockSpec((1, d), lambda i, j: (0, 0)),
            pl.BlockSpec((1, d), lambda i, j: (0, 0)),
            pl.BlockSpec((d, tn), lambda i, j: (0, j)),
        ],
        out_specs=[
            pl.BlockSpec((tm, tn), lambda i, j: (i, 0)),
            pl.BlockSpec((tm, tn), lambda i, j: (i, jnp.maximum(j - 1, 0))),
        ],
        out_shape=[
            jax.ShapeDtypeStruct((m, pool_width), F32),
            jax.ShapeDtypeStruct((m, n - pool_width), BF16),
        ],
        scratch_shapes=[pltpu.VMEM((tm, d), F32), pltpu.VMEM((2, tm, d), BF16),
                        pltpu.SemaphoreType.DMA(())],
        compiler_params=pltpu.CompilerParams(
            dimension_semantics=("arbitrary", "arbitrary"),
            vmem_limit_bytes=VMEM_LIMIT_BYTES),
        name="in_proj",
    )(x2, g, b, w_in)


def _attn_kernel(*refs, n_cast):
    q_ref, k_ref, v_ref = refs[:3]
    cast_in = refs[3:3 + n_cast]
    o_ref = refs[3 + n_cast]
    cast_out = refs[4 + n_cast:4 + 2 * n_cast]
    carry_scr, acc_scr, w_scr, a_scr, cum_scr = refs[4 + 2 * n_cast:]
    cum_scr[...] = jnp.where(
        lax.broadcasted_iota(jnp.int32, cum_scr.shape, 0)
        >= lax.broadcasted_iota(jnp.int32, cum_scr.shape, 1), 1.0, 0.0).astype(BF16)
    for src, dst in zip(cast_in, cast_out):
        dst[...] = src[...].astype(dst.dtype)
    qi = pl.program_id(2)
    tq, tk = ATTN_TQ, ATTN_TK
    nh = q_ref.shape[1] // HEAD_DIM
    skew = ATTN_SKEW
    heads = range(nh)
    cols = [slice(h * HEAD_DIM, (h + 1) * HEAD_DIM) for h in heads]

    def key_rows(kt):
        return pl.ds(pl.multiple_of(kt * tk, tk), tk)

    def scores(kt, h, q_rows=slice(None)):
        return lax.dot_general(q_ref[q_rows, cols[h]], k_ref[key_rows(kt), cols[h]],
                               (((1,), (1,)), ((), ())),
                               preferred_element_type=F32)

    def deferred_values(kt, j):
        h = nh - skew + j
        acc_scr[h] += jnp.dot(a_scr[j], v_ref[key_rows(kt), cols[h]],
                              preferred_element_type=F32)

    def sweep(kts, key_offsets=None):
        first = key_offsets is not None
        n_items = len(kts) * nh
        item = lambda n: (kts[n // nh], n % nh)
        kt_after = jnp.maximum(kts[-1] - 1, 0)
        if first:
            below = (lax.broadcasted_iota(jnp.int32, (tk, tk), 1)
                     < lax.broadcasted_iota(jnp.int32, (tk, tk), 0))

        def live_rows(n):
            return slice(key_offsets[n // nh], tq) if first else slice(None)

        def masked(n, x):
            off = key_offsets[n // nh]
            parts = []
            for r in range(off // tk, tq // tk):
                rows = x[r * tk - off:(r + 1) * tk - off]
                if r * tk == off:
                    parts.append(jnp.where(below, rows, 0.0))
                else:
                    parts.append(rows)
            return jnp.concatenate(parts, axis=0)

        def full_rows(n, x):
            off = key_offsets[n // nh] if first else 0
            if off == 0:
                return x
            return jnp.concatenate([jnp.zeros((off, x.shape[1]), x.dtype), x], axis=0)

        def softplus_tile(n, w):
            sp = jnp.maximum(
                w, jnp.log(1.0 + jnp.exp2(jnp.minimum(w, EXP2_CLAMP))) * LOG2_E)
            if first:
                sp = masked(n, sp)
            return sp.astype(BF16)

        def weights_values(n, w, cs):
            kt, h = item(n)
            start = first and n < nh
            total = full_rows(n, jnp.broadcast_to(cs[:, 0:1], (cs.shape[0], HEAD_DIM)))
            if start:
                a = jnp.exp2(w - cs)
            else:
                carry = carry_scr[h]
                a = jnp.exp2(w - cs - jnp.concatenate([carry] * (tk // HEAD_DIM), axis=1))
            if first:
                a = masked(n, a)
            carry_scr[h] = total if start else carry + total
            if n >= n_items - skew:
                a_scr[n - (n_items - skew)] = full_rows(n, a.astype(BF16))
                if start:
                    acc_scr[h] = jnp.zeros((tq, HEAD_DIM), F32)
                return
            av = full_rows(n, jnp.dot(a.astype(BF16), v_ref[key_rows(kt), cols[h]],
                                      preferred_element_type=F32))
            if start:
                acc_scr[h] = av
            else:
                acc_scr[h] += av

        w, cs, w_after = {}, {}, {}
        for j in range(skew):
            w[j] = scores(*item(j), live_rows(j)) if first else w_scr[j]
        for n in range(n_items):
            if n + skew < n_items:
                w[n + skew] = scores(*item(n + skew), live_rows(n + skew))
            else:
                w_after[n + skew - n_items] = scores(kt_after, n + skew - n_items)
            cs[n] = jnp.dot(softplus_tile(n, w[n]), cum_scr[...], preferred_element_type=F32)
            if n >= ATTN_LAG:
                weights_values(n - ATTN_LAG, w.pop(n - ATTN_LAG), cs.pop(n - ATTN_LAG))
            if n < skew and not first:
                deferred_values(kts[0] + 1, n)
        for n in range(n_items - ATTN_LAG, n_items):
            weights_values(n, w.pop(n), cs.pop(n))
        for j in range(skew):
            w_scr[j] = w_after[j]

    top = (qi + 1) * (tq // tk) - 1
    sweep([top, top - 1], [tk, 0])

    def quad_body(n, _):
        kt = top - 2 - 4 * n
        sweep([kt, kt - 1, kt - 2, kt - 3])
        return 0

    lax.fori_loop(0, qi // 2, quad_body, 0)

    @pl.when(qi % 2 == 1)
    def _():
        sweep([qi * 0 + 1, qi * 0])
    for j in range(skew):
        deferred_values(0, j)
    for h in heads:
        o_ref[:, cols[h]] = acc_scr[h].astype(o_ref.dtype)


def _sb_attention(qkv, batch, seq, heads, weights_f32):
    m = qkv.shape[0]
    nh = ATTN_HEADS
    tq, tk = ATTN_TQ, ATTN_TK
    assert heads % nh == 0 and seq % tq == 0 and tq == 2 * tk
    groups = heads // nh
    nq = seq // tq
    width = nh * HEAD_DIM
    steps = batch * groups * nq
    step = lambda b, g, i: ((b * groups + g) * nq + i, 0)
    slabs = []
    for wgt in weights_f32:
        rows, ncol = wgt.shape
        assert rows % (steps * 16) == 0
        slabs.append(pl.BlockSpec((rows // steps, ncol), step))
    out = pl.pallas_call(
        functools.partial(_attn_kernel, n_cast=len(weights_f32)),
        grid=(batch, groups, nq),
        in_specs=[
            pl.BlockSpec((tq, width), lambda b, g, i: (b * nq + i, g)),
            pl.BlockSpec((seq, width), lambda b, g, i: (b, groups + g)),
            pl.BlockSpec((seq, width), lambda b, g, i: (b, 2 * groups + g)),
        ] + slabs,
        out_specs=[pl.BlockSpec((tq, width), lambda b, g, i: (b * nq + i, g))] + slabs,
        out_shape=[jax.ShapeDtypeStruct((m, heads * HEAD_DIM), BF16)]
        + [jax.ShapeDtypeStruct(wgt.shape, BF16) for wgt in weights_f32],
        scratch_shapes=[pltpu.VMEM((nh, tq, HEAD_DIM), F32), pltpu.VMEM((nh, tq, HEAD_DIM), F32),
                        pltpu.VMEM((ATTN_SKEW, tq, tk), F32), pltpu.VMEM((ATTN_SKEW, tq, tk), BF16),
                        pltpu.VMEM((tk, tk), BF16)],
        compiler_params=pltpu.CompilerParams(
            dimension_semantics=("arbitrary", "arbitrary", "arbitrary"),
            vmem_limit_bytes=VMEM_LIMIT_BYTES),
        name="sb_attn",
    )(qkv, qkv, qkv, *weights_f32)
    return out[0], out[1:]


def _mix_kernel(x_ref, up_ref, halo_ref, ysb_ref, wout_ref, wpool_ref, pscale_ref,
                ling_ref, linb_ref, l1g_ref, l1b_ref, h1_ref, h1b_ref, *, seq, alpha):
    i = pl.program_id(0)
    tm = x_ref.shape[0]
    gw = wpool_ref.shape[1]
    blk_in_seq = i % (seq // tm)
    halo = jnp.where(blk_in_seq == 0, 0.0, halo_ref[...])
    u = up_ref[...]
    ext = jnp.concatenate([halo, u], axis=0)
    t = blk_in_seq * tm + lax.broadcasted_iota(jnp.int32, (tm, 1), 0)
    parts = []
    for g, w in enumerate(POOL_WINDOWS):
        cols = slice(g * gw, (g + 1) * gw)
        s = ext[:, cols]
        d = 1
        while d < w:
            s = s + pltpu.roll(s, d, axis=0)
            d *= 2
        s = s[POOL_HALO:, :]
        count = jnp.minimum(t + 1, w).astype(F32)
        y = s * (1.0 / count) - u[:, cols]
        yp = jnp.dot(y.astype(BF16), wpool_ref[g], preferred_element_type=F32)
        parts.append((yp * pscale_ref[g:g + 1, :]).astype(BF16))
    mix_in = jnp.concatenate(parts + [ysb_ref[...]], axis=1)
    mix = jnp.dot(mix_in, wout_ref[...], preferred_element_type=F32)
    h = _layer_norm(x_ref[...], ling_ref[...], linb_ref[...])
    h1 = _layer_norm(alpha * h + mix, l1g_ref[...], l1b_ref[...])
    h1_ref[...] = h1
    h1b_ref[...] = h1.astype(BF16)


def _mix_ln1(x2, u_pool, y_sb, w_out_bf16, w_pool_bf16, pool_scale, ln_in_g, ln_in_b,
             ln1_g, ln1_b, seq, alpha):
    m, d = x2.shape
    pw = u_pool.shape[1]
    sw = y_sb.shape[1]
    tm = MIX_TM
    assert seq % tm == 0 and tm % POOL_HALO == 0 and max(POOL_WINDOWS) <= POOL_HALO
    halo_blocks = tm // POOL_HALO
    const = lambda i: (0, 0)
    return pl.pallas_call(
        functools.partial(_mix_kernel, seq=seq, alpha=alpha),
        grid=(m // tm,),
        in_specs=[
            pl.BlockSpec((tm, d), lambda i: (i, 0)),
            pl.BlockSpec((tm, pw), lambda i: (i, 0)),
            pl.BlockSpec((POOL_HALO, pw), lambda i: (jnp.maximum(i * halo_blocks - 1, 0), 0)),
            pl.BlockSpec((tm, sw), lambda i: (i, 0)),
            pl.BlockSpec(w_out_bf16.shape, const, pipeline_mode=pl.Buffered(1)),
            pl.BlockSpec(w_pool_bf16.shape, lambda i: (0, 0, 0)),
            pl.BlockSpec(pool_scale.shape, const),
            pl.BlockSpec((1, d), const),
            pl.BlockSpec((1, d), const),
            pl.BlockSpec((1, d), const),
            pl.BlockSpec((1, d), const),
        ],
        out_specs=[pl.BlockSpec((tm, d), lambda i: (i, 0))] * 2,
        out_shape=[jax.ShapeDtypeStruct((m, d), F32), jax.ShapeDtypeStruct((m, d), BF16)],
        compiler_params=pltpu.CompilerParams(
            dimension_semantics=("arbitrary",),
            vmem_limit_bytes=VMEM_LIMIT_BYTES),
        name="mix_ln1",
    )(x2, u_pool, u_pool, y_sb, w_out_bf16, w_pool_bf16, pool_scale,
      ln_in_g, ln_in_b, ln1_g, ln1_b)


def _ffn_kernel(h1_ref, h1b_ref, w1_ref, b1_ref, w2_ref, b2_ref, g_ref, b_ref, o_ref,
                acc_scr, *, alpha):
    i = pl.program_id(0)
    f = pl.program_id(1)
    last_i = pl.num_programs(0) - 1
    last_f = pl.num_programs(1) - 1

    def mlp_chunk():
        t = jnp.dot(h1b_ref[...], w1_ref[...], preferred_element_type=F32) + b1_ref[...]
        t = jnp.maximum(t, 0.0)
        return jnp.dot((t * t).astype(BF16), w2_ref[...], preferred_element_type=F32)

    def norm():
        return _layer_norm(acc_scr[...], g_ref[...], b_ref[...])

    @pl.when((f == 0) & (i == 0))
    def _():
        acc_scr[...] = mlp_chunk()

    @pl.when((f == 0) & (i > 0))
    def _():
        o_ref[...] = norm()
        acc_scr[...] = mlp_chunk()

    @pl.when((f > 0) & (f < last_f))
    def _():
        acc_scr[...] += mlp_chunk()

    @pl.when(f == last_f)
    def _():
        acc_scr[...] = alpha * h1_ref[...] + (acc_scr[...] + mlp_chunk() + b2_ref[...])

    @pl.when((f == last_f) & (i == last_i))
    def _():
        o_ref[...] = norm()


def _ffn_ln2(h1, h1_bf16, w1_bf16, b1, w2_bf16, b2, g, b, alpha):
    m, d = h1.shape
    dff = w1_bf16.shape[1]
    tm, tf = FFN_TM, FFN_TF
    assert m % tm == 0 and dff % tf == 0 and dff // tf >= 2
    n_i, n_f = m // tm, dff // tf
    const = lambda i, f: (0, 0)

    def out_block(i, f):
        final = (i == n_i - 1) & (f == n_f - 1)
        return (jnp.where(final, i, jnp.maximum(i - 1, 0)), 0)

    return pl.pallas_call(
        functools.partial(_ffn_kernel, alpha=alpha),
        grid=(n_i, n_f),
        in_specs=[
            pl.BlockSpec((tm, d), lambda i, f: (i, 0)),
            pl.BlockSpec((tm, d), lambda i, f: (i, 0)),
            pl.BlockSpec((d, tf), lambda i, f: (0, f)),
            pl.BlockSpec((1, tf), lambda i, f: (0, f)),
            pl.BlockSpec((tf, d), lambda i, f: (f, 0)),
            pl.BlockSpec((1, d), const),
            pl.BlockSpec((1, d), const),
            pl.BlockSpec((1, d), const),
        ],
        out_specs=pl.BlockSpec((tm, d), out_block),
        out_shape=jax.ShapeDtypeStruct((m, d), F32),
        scratch_shapes=[pltpu.VMEM((tm, d), F32)],
        compiler_params=pltpu.CompilerParams(
            dimension_semantics=("arbitrary", "arbitrary"),
            vmem_limit_bytes=VMEM_LIMIT_BYTES),
        name="ffn_ln2",
    )(h1, h1_bf16, w1_bf16, b1, w2_bf16, b2, g, b)


def kernel(x, ln_in_g, ln_in_b, w_in, w_pool, pool_scale, w_out, ln1_g, ln1_b,
           w_ff1, b_ff1, w_ff2, b_ff2, ln2_g, ln2_b):
    batch, seq, d = x.shape
    depth, groups, gw, _ = w_pool.shape
    assert depth == 1, "single trunk layer"
    pool_width = groups * gw
    sb_width = (w_in.shape[2] - pool_width) // 3
    heads = sb_width // HEAD_DIM
    alpha = float((2.0 * depth) ** 0.25)
    m = batch * seq

    x2 = x.reshape(m, d)
    row = lambda p: p.reshape(1, -1)

    q_scale = float(LOG2_E / np.sqrt(np.float32(HEAD_DIM)))
    u_pool, qkv = _in_proj(x2, row(ln_in_g), row(ln_in_b), w_in[0], pool_width, q_scale)
    y_sb, (w_out_bf16, w_ff1_bf16, w_ff2_bf16, w_pool_bf16) = _sb_attention(
        qkv, batch, seq, heads,
        [w_out[0], w_ff1[0], w_ff2[0], w_pool[0].reshape(groups * gw, gw)])
    h1, h1_bf16 = _mix_ln1(x2, u_pool, y_sb, w_out_bf16, w_pool_bf16.reshape(groups, gw, gw),
                  pool_scale[0], row(ln_in_g), row(ln_in_b), row(ln1_g[0]), row(ln1_b[0]),
                  seq, alpha)
    out = _ffn_ln2(h1, h1_bf16, w_ff1_bf16, row(b_ff1[0]), w_ff2_bf16,
                   row(b_ff2[0]), row(ln2_g[0]), row(ln2_b[0]), alpha)
    return out.reshape(batch, seq, d)
```

```python
import functools

import jax
import jax.numpy as jnp
import numpy as np
from jax import lax
from jax.experimental import pallas as pl
from jax.experimental.pallas import tpu as pltpu

F32 = jnp.float32
BF16 = jnp.bfloat16

LN_EPS = 1e-5
LOG2_E = 1.4426950408889634
EXP2_CLAMP = 100.0
POOL_WINDOWS = (2, 4, 8, 16)
POOL_HALO = 16
HEAD_DIM = 128

VMEM_LIMIT_BYTES = 56 * 1024 * 1024

INPROJ_TM = 1024
INPROJ_TN = 1024
ATTN_TQ = 512
ATTN_TK = 256
ATTN_HEADS = 4
ATTN_SKEW = 1
ATTN_LAG = 2
MIX_TM = 512
FFN_TM = 512
FFN_TF = 1024
LN_ROWS = 128


def _layer_norm(x, g, b):
    mu = jnp.mean(x, axis=-1, keepdims=True)
    xc = x - mu
    var = jnp.mean(xc * xc, axis=-1, keepdims=True)
    return xc * lax.rsqrt(var + LN_EPS) * g + b


def _inproj_kernel(x_hbm, g_ref, b_ref, w_ref, pool_ref, qkv_ref, x_scr, h_scr, sem, *, q_scale):
    i = pl.program_id(0)
    j = pl.program_id(1)
    n_i = pl.num_programs(0)
    last_j = pl.num_programs(1) - 1
    tm = x_scr.shape[0]

    def x_copy(blk):
        return pltpu.make_async_copy(x_hbm.at[pl.ds(blk * tm, tm), :], x_scr, sem)

    def normalise(slot):
        for r in range(tm // LN_ROWS):
            rows = slice(r * LN_ROWS, (r + 1) * LN_ROWS)
            h_scr[slot, rows, :] = _layer_norm(
                x_scr[rows, :], g_ref[...], b_ref[...]).astype(BF16)

    def project(h):
        return jnp.dot(h, w_ref[...].astype(BF16), preferred_element_type=F32)

    @pl.when((i == 0) & (j == 0))
    def _():
        x_copy(0).start()
        x_copy(0).wait()
        normalise(0)

        @pl.when(n_i > 1)
        def _():
            x_copy(1).start()

    @pl.when(j < last_j)
    def _():
        acc = project(h_scr[i % 2])

        @pl.when(j == 0)
        def _():
            pool_ref[...] = acc

        @pl.when(j == 1)
        def _():
            qkv_ref[...] = (acc * q_scale).astype(BF16)

        @pl.when(j > 1)
        def _():
            qkv_ref[...] = acc.astype(BF16)

    def last_step_and_next_norm(slot):
        x_copy(i + 1).wait()
        acc = project(h_scr[slot])
        normalise(1 - slot)
        qkv_ref[...] = acc.astype(BF16)

        @pl.when(i + 2 < n_i)
        def _():
            x_copy(i + 2).start()

    for parity in (0, 1):
        @pl.when((j == last_j) & (i < n_i - 1) & (i % 2 == parity))
        def _():
            last_step_and_next_norm(parity)

    @pl.when((j == last_j) & (i == n_i - 1))
    def _():
        qkv_ref[...] = project(h_scr[i % 2]).astype(BF16)


def _in_proj(x2, g, b, w_in, pool_width, q_scale):
    m, d = x2.shape
    n = w_in.shape[1]
    tm, tn = INPROJ_TM, INPROJ_TN
    assert pool_width == tn and n == 4 * tn and m % tm == 0
    return pl.pallas_call(
        functools.partial(_inproj_kernel, q_scale=q_scale),
        grid=(m // tm, n // tn),
        in_specs=[
            pl.BlockSpec(memory_space=pl.ANY),
            pl.BlockSpec((1, d), lambda i, j: (0, 0)),
            pl.BlockSpec((1, d), lambda i, j: (0, 0)),
            pl.BlockSpec((d, tn), lambda i, j: (0, j)),
        ],
        out_specs=[
            pl.BlockSpec((tm, tn), lambda i, j: (i, 0)),
            pl.BlockSpec((tm, tn), lambda i, j: (i, jnp.maximum(j - 1, 0))),
        ],
        out_shape=[
            jax.ShapeDtypeStruct((m, pool_width), F32),
            jax.ShapeDtypeStruct((m, n - pool_width), BF16),
        ],
        scratch_shapes=[pltpu.VMEM((tm, d), F32), pltpu.VMEM((2, tm, d), BF16),
                        pltpu.SemaphoreType.DMA(())],
        compiler_params=pltpu.CompilerParams(
            dimension_semantics=("arbitrary", "arbitrary"),
            vmem_limit_bytes=VMEM_LIMIT_BYTES),
        name="in_proj",
    )(x2, g, b, w_in)


def _attn_kernel(*refs, n_cast):
    q_ref, k_ref, v_ref = refs[:3]
    cast_in = refs[3:3 + n_cast]
    o_ref = refs[3 + n_cast]
    cast_out = refs[4 + n_cast:4 + 2 * n_cast]
    carry_scr, acc_scr, w_scr, a_scr, cum_scr = refs[4 + 2 * n_cast:]
    cum_scr[...] = jnp.where(
        lax.broadcasted_iota(jnp.int32, cum_scr.shape, 0)
        >= lax.broadcasted_iota(jnp.int32, cum_scr.shape, 1), 1.0, 0.0).astype(BF16)
    for src, dst in zip(cast_in, cast_out):
        dst[...] = src[...].astype(dst.dtype)
    qi = pl.program_id(2)
    tq, tk = ATTN_TQ, ATTN_TK
    nh = q_ref.shape[1] // HEAD_DIM
    skew = ATTN_SKEW
    heads = range(nh)
    cols = [slice(h * HEAD_DIM, (h + 1) * HEAD_DIM) for h in heads]

    def key_rows(kt):
        return pl.ds(pl.multiple_of(kt * tk, tk), tk)

    def scores(kt, h, q_rows=slice(None)):
        return lax.dot_general(q_ref[q_rows, cols[h]], k_ref[key_rows(kt), cols[h]],
                               (((1,), (1,)), ((), ())),
                               preferred_element_type=F32)

    def deferred_values(kt, j):
        h = nh - skew + j
        acc_scr[h] += jnp.dot(a_scr[j], v_ref[key_rows(kt), cols[h]],
                              preferred_element_type=F32)

    def sweep(kts, key_offsets=None):
        first = key_offsets is not None
        n_items = len(kts) * nh
        item = lambda n: (kts[n // nh], n % nh)
        kt_after = jnp.maximum(kts[-1] - 1, 0)
        if first:
            below = (lax.broadcasted_iota(jnp.int32, (tk, tk), 1)
                     < lax.broadcasted_iota(jnp.int32, (tk, tk), 0))

        def live_rows(n):
            return slice(key_offsets[n // nh], tq) if first else slice(None)

        def masked(n, x):
            off = key_offsets[n // nh]
            parts = []
            for r in range(off // tk, tq // tk):
                rows = x[r * tk - off:(r + 1) * tk - off]
                if r * tk == off:
                    parts.append(jnp.where(below, rows, 0.0))
                else:
                    parts.append(rows)
            return jnp.concatenate(parts, axis=0)

        def full_rows(n, x):
            off = key_offsets[n // nh] if first else 0
            if off == 0:
                return x
            return jnp.concatenate([jnp.zeros((off, x.shape[1]), x.dtype), x], axis=0)

        def softplus_tile(n, w):
            sp = jnp.maximum(
                w, jnp.log(1.0 + jnp.exp2(jnp.minimum(w, EXP2_CLAMP))) * LOG2_E)
            if first:
                sp = masked(n, sp)
            return sp.astype(BF16)

        def weights_values(n, w, cs):
            kt, h = item(n)
            start = first and n < nh
            total = full_rows(n, jnp.broadcast_to(cs[:, 0:1], (cs.shape[0], HEAD_DIM)))
            if start:
                a = jnp.exp2(w - cs)
            else:
                carry = carry_scr[h]
                a = jnp.exp2(w - cs - jnp.concatenate([carry] * (tk // HEAD_DIM), axis=1))
            if first:
                a = masked(n, a)
            carry_scr[h] = total if start else carry + total
            if n >= n_items - skew:
                a_scr[n - (n_items - skew)] = full_rows(n, a.astype(BF16))
                if start:
                    acc_scr[h] = jnp.zeros((tq, HEAD_DIM), F32)
                return
            av = full_rows(n, jnp.dot(a.astype(BF16), v_ref[key_rows(kt), cols[h]],
                                      preferred_element_type=F32))
            if start:
                acc_scr[h] = av
            else:
                acc_scr[h] += av

        w, cs, w_after = {}, {}, {}
        for j in range(skew):
            w[j] = scores(*item(j), live_rows(j)) if first else w_scr[j]
        for n in range(n_items):
            if n + skew < n_items:
                w[n + skew] = scores(*item(n + skew), live_rows(n + skew))
            else:
                w_after[n + skew - n_items] = scores(kt_after, n + skew - n_items)
            cs[n] = jnp.dot(softplus_tile(n, w[n]), cum_scr[...], preferred_element_type=F32)
            if n >= ATTN_LAG:
                weights_values(n - ATTN_LAG, w.pop(n - ATTN_LAG), cs.pop(n - ATTN_LAG))
            if n < skew and not first:
                deferred_values(kts[0] + 1, n)
        for n in range(n_items - ATTN_LAG, n_items):
            weights_values(n, w.pop(n), cs.pop(n))
        for j in range(skew):
            w_scr[j] = w_after[j]

    top = (qi + 1) * (tq // tk) - 1
    sweep([top, top - 1], [tk, 0])

    def quad_body(n, _):
        kt = top - 2 - 4 * n
        sweep([kt, kt - 1, kt - 2, kt - 3])
        return 0

    lax.fori_loop(0, qi // 2, quad_body, 0)

    @pl.when(qi % 2 == 1)
    def _():
        sweep([qi * 0 + 1, qi * 0])
    for j in range(skew):
        deferred_values(0, j)
    for h in heads:
        o_ref[:, cols[h]] = acc_scr[h].astype(o_ref.dtype)


def _sb_attention(qkv, batch, seq, heads, weights_f32):
    m = qkv.shape[0]
    nh = ATTN_HEADS
    tq, tk = ATTN_TQ, ATTN_TK
    assert heads % nh == 0 and seq % tq == 0 and tq == 2 * tk
    groups = heads // nh
    nq = seq // tq
    width = nh * HEAD_DIM
    steps = batch * groups * nq
    step = lambda b, g, i: ((b * groups + g) * nq + i, 0)
    slabs = []
    for wgt in weights_f32:
        rows, ncol = wgt.shape
        assert rows % (steps * 16) == 0
        slabs.append(pl.BlockSpec((rows // steps, ncol), step))
    out = pl.pallas_call(
        functools.partial(_attn_kernel, n_cast=len(weights_f32)),
        grid=(batch, groups, nq),
        in_specs=[
            pl.BlockSpec((tq, width), lambda b, g, i: (b * nq + i, g)),
            pl.BlockSpec((seq, width), lambda b, g, i: (b, groups + g)),
            pl.BlockSpec((seq, width), lambda b, g, i: (b, 2 * groups + g)),
        ] + slabs,
        out_specs=[pl.BlockSpec((tq, width), lambda b, g, i: (b * nq + i, g))] + slabs,
        out_shape=[jax.ShapeDtypeStruct((m, heads * HEAD_DIM), BF16)]
        + [jax.ShapeDtypeStruct(wgt.shape, BF16) for wgt in weights_f32],
        scratch_shapes=[pltpu.VMEM((nh, tq, HEAD_DIM), F32), pltpu.VMEM((nh, tq, HEAD_DIM), F32),
                        pltpu.VMEM((ATTN_SKEW, tq, tk), F32), pltpu.VMEM((ATTN_SKEW, tq, tk), BF16),
                        pltpu.VMEM((tk, tk), BF16)],
        compiler_params=pltpu.CompilerParams(
            dimension_semantics=("arbitrary", "arbitrary", "arbitrary"),
            vmem_limit_bytes=VMEM_LIMIT_BYTES),
        name="sb_attn",
    )(qkv, qkv, qkv, *weights_f32)
    return out[0], out[1:]


def _mix_kernel(x_ref, up_ref, halo_ref, ysb_ref, wout_ref, wpool_ref, pscale_ref,
                ling_ref, linb_ref, l1g_ref, l1b_ref, h1_ref, h1b_ref, *, seq, alpha):
    i = pl.program_id(0)
    tm = x_ref.shape[0]
    gw = wpool_ref.shape[1]
    blk_in_seq = i % (seq // tm)
    halo = jnp.where(blk_in_seq == 0, 0.0, halo_ref[...])
    u = up_ref[...]
    ext = jnp.concatenate([halo, u], axis=0)
    t = blk_in_seq * tm + lax.broadcasted_iota(jnp.int32, (tm, 1), 0)
    parts = []
    for g, w in enumerate(POOL_WINDOWS):
        cols = slice(g * gw, (g + 1) * gw)
        s = ext[:, cols]
        d = 1
        while d < w:
            s = s + pltpu.roll(s, d, axis=0)
            d *= 2
        s = s[POOL_HALO:, :]
        count = jnp.minimum(t + 1, w).astype(F32)
        y = s * (1.0 / count) - u[:, cols]
        yp = jnp.dot(y.astype(BF16), wpool_ref[g], preferred_element_type=F32)
        parts.append((yp * pscale_ref[g:g + 1, :]).astype(BF16))
    mix_in = jnp.concatenate(parts + [ysb_ref[...]], axis=1)
    mix = jnp.dot(mix_in, wout_ref[...], preferred_element_type=F32)
    for c in range(tm // LN_ROWS):
        rows = slice(c * LN_ROWS, (c + 1) * LN_ROWS)
        h = _layer_norm(x_ref[rows, :], ling_ref[...], linb_ref[...])
        h1 = _layer_norm(alpha * h + mix[rows, :], l1g_ref[...], l1b_ref[...])
        h1_ref[rows, :] = h1
        h1b_ref[rows, :] = h1.astype(BF16)


def _mix_ln1(x2, u_pool, y_sb, w_out_bf16, w_pool_bf16, pool_scale, ln_in_g, ln_in_b,
             ln1_g, ln1_b, seq, alpha):
    m, d = x2.shape
    pw = u_pool.shape[1]
    sw = y_sb.shape[1]
    tm = MIX_TM
    assert seq % tm == 0 and tm % POOL_HALO == 0 and max(POOL_WINDOWS) <= POOL_HALO
    halo_blocks = tm // POOL_HALO
    const = lambda i: (0, 0)
    return pl.pallas_call(
        functools.partial(_mix_kernel, seq=seq, alpha=alpha),
        grid=(m // tm,),
        in_specs=[
            pl.BlockSpec((tm, d), lambda i: (i, 0)),
            pl.BlockSpec((tm, pw), lambda i: (i, 0)),
            pl.BlockSpec((POOL_HALO, pw), lambda i: (jnp.maximum(i * halo_blocks - 1, 0), 0)),
            pl.BlockSpec((tm, sw), lambda i: (i, 0)),
            pl.BlockSpec(w_out_bf16.shape, const, pipeline_mode=pl.Buffered(1)),
            pl.BlockSpec(w_pool_bf16.shape, lambda i: (0, 0, 0)),
            pl.BlockSpec(pool_scale.shape, const),
            pl.BlockSpec((1, d), const),
            pl.BlockSpec((1, d), const),
            pl.BlockSpec((1, d), const),
            pl.BlockSpec((1, d), const),
        ],
        out_specs=[pl.BlockSpec((tm, d), lambda i: (i, 0))] * 2,
        out_shape=[jax.ShapeDtypeStruct((m, d), F32), jax.ShapeDtypeStruct((m, d), BF16)],
        compiler_params=pltpu.CompilerParams(
            dimension_semantics=("arbitrary",),
            vmem_limit_bytes=VMEM_LIMIT_BYTES),
        name="mix_ln1",
    )(x2, u_pool, u_pool, y_sb, w_out_bf16, w_pool_bf16, pool_scale,
      ln_in_g, ln_in_b, ln1_g, ln1_b)


def _ffn_kernel(h1_ref, h1b_ref, w1_ref, b1_ref, w2_ref, b2_ref, g_ref, b_ref, o_ref,
                acc_scr, *, alpha):
    i = pl.program_id(0)
    f = pl.program_id(1)
    last_i = pl.num_programs(0) - 1
    last_f = pl.num_programs(1) - 1

    def mlp_chunk():
        t = jnp.dot(h1b_ref[...], w1_ref[...], preferred_element_type=F32) + b1_ref[...]
        t = jnp.maximum(t, 0.0)
        return jnp.dot((t * t).astype(BF16), w2_ref[...], preferred_element_type=F32)

    def norm():
        return _layer_norm(acc_scr[...], g_ref[...], b_ref[...])

    @pl.when((f == 0) & (i == 0))
    def _():
        acc_scr[...] = mlp_chunk()

    @pl.when((f == 0) & (i > 0))
    def _():
        o_ref[...] = norm()
        acc_scr[...] = mlp_chunk()

    @pl.when((f > 0) & (f < last_f))
    def _():
        acc_scr[...] += mlp_chunk()

    @pl.when(f == last_f)
    def _():
        acc_scr[...] = alpha * h1_ref[...] + (acc_scr[...] + mlp_chunk() + b2_ref[...])

    @pl.when((f == last_f) & (i == last_i))
    def _():
        o_ref[...] = norm()


def _ffn_ln2(h1, h1_bf16, w1_bf16, b1, w2_bf16, b2, g, b, alpha):
    m, d = h1.shape
    dff = w1_bf16.shape[1]
    tm, tf = FFN_TM, FFN_TF
    assert m % tm == 0 and dff % tf == 0 and dff // tf >= 2
    n_i, n_f = m // tm, dff // tf
    const = lambda i, f: (0, 0)

    def out_block(i, f):
        final = (i == n_i - 1) & (f == n_f - 1)
        return (jnp.where(final, i, jnp.maximum(i - 1, 0)), 0)

    return pl.pallas_call(
        functools.partial(_ffn_kernel, alpha=alpha),
        grid=(n_i, n_f),
        in_specs=[
            pl.BlockSpec((tm, d), lambda i, f: (i, 0)),
            pl.BlockSpec((tm, d), lambda i, f: (i, 0)),
            pl.BlockSpec((d, tf), lambda i, f: (0, f)),
            pl.BlockSpec((1, tf), lambda i, f: (0, f)),
            pl.BlockSpec((tf, d), lambda i, f: (f, 0)),
            pl.BlockSpec((1, d), const),
            pl.BlockSpec((1, d), const),
            pl.BlockSpec((1, d), const),
        ],
        out_specs=pl.BlockSpec((tm, d), out_block),
        out_shape=jax.ShapeDtypeStruct((m, d), F32),
        scratch_shapes=[pltpu.VMEM((tm, d), F32)],
        compiler_params=pltpu.CompilerParams(
            dimension_semantics=("arbitrary", "arbitrary"),
            vmem_limit_bytes=VMEM_LIMIT_BYTES),
        name="ffn_ln2",
    )(h1, h1_bf16, w1_bf16, b1, w2_bf16, b2, g, b)


def kernel(x, ln_in_g, ln_in_b, w_in, w_pool, pool_scale, w_out, ln1_g, ln1_b,
           w_ff1, b_ff1, w_ff2, b_ff2, ln2_g, ln2_b):
    batch, seq, d = x.shape
    depth, groups, gw, _ = w_pool.shape
    assert depth == 1, "single trunk layer"
    pool_width = groups * gw
    sb_width = (w_in.shape[2] - pool_width) // 3
    heads = sb_width // HEAD_DIM
    alpha = float((2.0 * depth) ** 0.25)
    m = batch * seq

    x2 = x.reshape(m, d)
    row = lambda p: p.reshape(1, -1)

    q_scale = float(LOG2_E / np.sqrt(np.float32(HEAD_DIM)))
    u_pool, qkv = _in_proj(x2, row(ln_in_g), row(ln_in_b), w_in[0], pool_width, q_scale)
    y_sb, (w_out_bf16, w_ff1_bf16, w_ff2_bf16, w_pool_bf16) = _sb_attention(
        qkv, batch, seq, heads,
        [w_out[0], w_ff1[0], w_ff2[0], w_pool[0].reshape(groups * gw, gw)])
    h1, h1_bf16 = _mix_ln1(x2, u_pool, y_sb, w_out_bf16, w_pool_bf16.reshape(groups, gw, gw),
                  pool_scale[0], row(ln_in_g), row(ln_in_b), row(ln1_g[0]), row(ln1_b[0]),
                  seq, alpha)
    out = _ffn_ln2(h1, h1_bf16, w_ff1_bf16, row(b_ff1[0]), w_ff2_bf16,
                   row(b_ff2[0]), row(ln2_g[0]), row(ln2_b[0]), alpha)
    return out.reshape(batch, seq, d)
```
